```python
import jax, jax.numpy as jnp
from jax import lax
import numpy as np

D_MODEL = 4096
BATCH = 1
SEQ = 8192
DEPTH = 1
DEC_BATCH = 8
DEC_SEQ = 32
PAST_LEN = 1024

CHUNK = 64
A_WIDTH = D_MODEL // 2
A_GROUPS = 4
A_GROUP_DIM = A_WIDTH // A_GROUPS
A_CHUNK = 128
B_WIDTH = D_MODEL - A_WIDTH
B_HEAD_DIM = 128
B_HEADS = B_WIDTH // B_HEAD_DIM
B_PAST_CHUNKS = 8
B_PAST_REACH = B_PAST_CHUNKS * CHUNK
B_BAND = B_PAST_REACH + CHUNK
REL_MAX = 256
REL_SIZE = REL_MAX + CHUNK
MIX_WIDTH = A_WIDTH + B_WIDTH
IN_COLS = 2 * A_WIDTH + 3 * B_WIDTH
SPLITS = (A_WIDTH, 2 * A_WIDTH, 2 * A_WIDTH + B_WIDTH, 2 * A_WIDTH + 2 * B_WIDTH)
N_GROUPS = 8
EXPERTS_PER_GROUP = 8
N_EXPERTS = N_GROUPS * EXPERTS_PER_GROUP
EXPERT_TOP_K = 2
D_EXPERT = D_MODEL // 4
MOE_BLOCK = 128
EPS = 1e-6
NEG = -1e30

kernel_name = 'hymba_stream_sgu_bandattn_hmoe'


def _rmsnorm(x, g):
    x32 = x.astype(jnp.float32)
    y = x32 * lax.rsqrt(jnp.mean(x32 * x32, axis=-1, keepdims=True) + EPS)
    return (y * g.astype(jnp.float32)).astype(x.dtype)


def _project(x, g_mix, w_in, v_gain):
    lead = x.shape[:-1]
    p = _rmsnorm(x, g_mix) @ w_in
    u, va, q, k, vb = jnp.split(p, SPLITS, axis=-1)
    u = jax.nn.gelu(u).reshape(*lead, A_GROUPS, A_GROUP_DIM)
    va = _rmsnorm(jax.nn.gelu(va).reshape(*lead, A_GROUPS, A_GROUP_DIM),
                  v_gain.reshape(A_GROUPS, A_GROUP_DIM))
    heads = lambda t: t.reshape(*lead, B_HEADS, B_HEAD_DIM)
    return u, va, heads(q), heads(k), heads(vb)


def _masked_spatial(w_s):
    return w_s * jnp.tril(jnp.ones((A_CHUNK, A_CHUNK), w_s.dtype))


def _sgu_prompt(u, va, w_s, b_s):
    b, s = u.shape[:2]
    vc = va.reshape(b, s // A_CHUNK, A_CHUNK, A_GROUPS, A_GROUP_DIM)
    f = jnp.einsum('gij,bnjgc->bnigc', _masked_spatial(w_s), vc) + b_s.T[:, :, None]
    return u * f.reshape(u.shape)


def _sgu_sample(u, va, w_s, b_s):
    t = u.shape[1]
    w = _masked_spatial(w_s)[:, :t, :t]
    f = jnp.einsum('gij,bjgc->bigc', w, va) + b_s[:, :t].T[:, :, None]
    return u * f


def _rel_bias(table, n_q, n_k, offset):
    d = jnp.arange(n_q)[:, None] + offset - jnp.arange(n_k)[None, :]
    idx = jnp.clip(d, -(CHUNK - 1), REL_MAX) + (CHUNK - 1)
    return table[:, idx].astype(jnp.float32)


def _attn_prompt(q, k, v, table):
    b, s = q.shape[:2]
    n_chunks = s // CHUNK
    scale = B_HEAD_DIM ** -0.5
    pad = ((0, 0), (B_PAST_REACH, 0), (0, 0), (0, 0))
    kp, vp = jnp.pad(k, pad), jnp.pad(v, pad)
    bias = _rel_bias(table, CHUNK, B_BAND, B_PAST_REACH)
    slot = jnp.arange(B_BAND)

    def one_chunk(c):
        qc = lax.dynamic_slice_in_dim(q, c * CHUNK, CHUNK, axis=1)
        kb = lax.dynamic_slice_in_dim(kp, c * CHUNK, B_BAND, axis=1)
        vb = lax.dynamic_slice_in_dim(vp, c * CHUNK, B_BAND, axis=1)
        sc = jnp.einsum('bqhd,bkhd->bhqk', qc, kb).astype(jnp.float32) * scale + bias
        sc = jnp.where(slot >= (B_PAST_CHUNKS - c) * CHUNK, sc, NEG)
        p = jax.nn.softmax(sc, axis=-1).astype(v.dtype)
        return jnp.einsum('bhqk,bkhd->bqhd', p, vb)

    o = lax.map(one_chunk, jnp.arange(n_chunks))
    return o.transpose(1, 0, 2, 3, 4).reshape(b, s, B_HEADS, B_HEAD_DIM)


def _attn_sample(q, k, v, cache_k, cache_v):
    return None


def _attn_sample_bias(q, k, v, cache_k, cache_v, table):
    L = cache_k.shape[1]
    t = q.shape[1]
    scale = B_HEAD_DIM ** -0.5
    ka = jnp.concatenate([cache_k.astype(k.dtype), k], axis=1)
    va = jnp.concatenate([cache_v.astype(v.dtype), v], axis=1)
    sc = jnp.einsum('bqhd,bkhd->bhqk', q, ka).astype(jnp.float32) * scale + _rel_bias(table, t, L + t, L)
    p = jax.nn.softmax(sc, axis=-1).astype(v.dtype)
    return jnp.einsum('bhqk,bkhd->bqhd', p, va)


def _merge(a, o, g_a, g_b, w_out):
    lead = a.shape[:-2]
    a = _rmsnorm(a.reshape(*lead, A_WIDTH), g_a)
    o = _rmsnorm(o.reshape(*lead, B_WIDTH), g_b)
    return jnp.concatenate([a, o], axis=-1) @ w_out


def _hier_moe(x, w_rg, b_rg, w_re, b_re, w_gate, w_up, w_down):
    shp = x.shape
    xt = x.reshape(-1, D_MODEL)
    n_tok = xt.shape[0]
    x32 = xt.astype(jnp.float32)
    pg = jax.nn.softmax(x32 @ w_rg.astype(jnp.float32) + b_rg.astype(jnp.float32), axis=-1)
    pg_top, g_sel = lax.top_k(pg, 1)
    le = (x32 @ w_re.astype(jnp.float32) + b_re.astype(jnp.float32)).reshape(n_tok, N_GROUPS, EXPERTS_PER_GROUP)
    le_sel = le[jnp.arange(n_tok), g_sel[:, 0]]
    pe_top, j_sel = lax.top_k(jax.nn.softmax(le_sel, axis=-1), EXPERT_TOP_K)
    gates = pg_top * pe_top / jnp.sum(pe_top, axis=-1, keepdims=True)
    expert_id = g_sel * EXPERTS_PER_GROUP + j_sel

    n_assign = n_tok * EXPERT_TOP_K
    flat_e = expert_id.reshape(-1)
    flat_tok = jnp.repeat(jnp.arange(n_tok, dtype=jnp.int32), EXPERT_TOP_K)
    flat_w = gates.reshape(-1)
    order = jnp.argsort(flat_e)
    se, stok, sw = flat_e[order], flat_tok[order], flat_w[order]

    per_expert = -(-n_assign // N_EXPERTS)
    block = min(MOE_BLOCK, max(8, 1 << (per_expert - 1).bit_length()))
    n_rows = -(-n_assign // block) * block + N_EXPERTS * block
    n_blocks = n_rows // block
    counts = jnp.bincount(flat_e, length=N_EXPERTS)
    padded = (counts + block - 1) // block * block
    start = jnp.cumsum(counts) - counts
    pend = jnp.cumsum(padded)
    pstart = pend - padded
    dest = pstart[se] + jnp.arange(n_assign) - start[se]
    buf_tok = jnp.zeros((n_rows,), jnp.int32).at[dest].set(stok)
    buf_w = jnp.zeros((n_rows,), jnp.float32).at[dest].set(sw)
    blk_e = jnp.clip(jnp.searchsorted(pend, jnp.arange(n_blocks) * block, side='right'), 0, N_EXPERTS - 1)

    def one_block(args):
        tok, w, e = args
        xb = xt[tok]
        h = jax.nn.silu(xb @ w_gate[e]) * (xb @ w_up[e])
        return (h @ w_down[e]) * w[:, None].astype(xt.dtype)

    yb = lax.map(one_block, (buf_tok.reshape(n_blocks, block), buf_w.reshape(n_blocks, block), blk_e))
    y = jnp.zeros_like(xt).at[buf_tok].add(yb.reshape(n_rows, D_MODEL))
    return y.reshape(shp)


def setup_inputs(seed: int = 0) -> dict:
    key = jax.random.key(seed)
    ks = jax.random.split(key, 24)
    nrm = lambda k, shape, scale: jax.random.normal(k, shape, jnp.float32) * scale
    gain = lambda k, shape: 1.0 + 0.02 * jax.random.normal(k, shape, jnp.float32)
    cache_len = min(B_PAST_REACH, PAST_LEN)
    return {
        'x_prompt': nrm(ks[0], (BATCH, SEQ, D_MODEL), 1.0),
        'x_sample': nrm(ks[1], (DEC_BATCH, DEC_SEQ, D_MODEL), 1.0),
        'cache_b_k': nrm(ks[2], (DEPTH, DEC_BATCH, cache_len, B_HEADS, B_HEAD_DIM), 1.0),
        'cache_b_v': nrm(ks[3], (DEPTH, DEC_BATCH, cache_len, B_HEADS, B_HEAD_DIM), 1.0),
        'g_mix': gain(ks[4], (DEPTH, D_MODEL)),
        'w_in': nrm(ks[5], (DEPTH, D_MODEL, IN_COLS), D_MODEL ** -0.5),
        'sgu_v_gain': gain(ks[6], (DEPTH, A_WIDTH)),
        'sgu_w': nrm(ks[7], (DEPTH, A_GROUPS, A_CHUNK, A_CHUNK), A_CHUNK ** -0.5),
        'sgu_b': 1.0 + 0.1 * jax.random.normal(ks[8], (DEPTH, A_GROUPS, A_CHUNK), jnp.float32),
        'rel_bias': nrm(ks[9], (DEPTH, B_HEADS, REL_SIZE), 0.1),
        'g_out_a': gain(ks[10], (DEPTH, A_WIDTH)),
        'g_out_b': gain(ks[11], (DEPTH, B_WIDTH)),
        'w_out': nrm(ks[12], (DEPTH, MIX_WIDTH, D_MODEL), MIX_WIDTH ** -0.5),
        'g_ffn': gain(ks[13], (DEPTH, D_MODEL)),
        'w_router_group': nrm(ks[14], (DEPTH, D_MODEL, N_GROUPS), D_MODEL ** -0.5),
        'b_router_group': nrm(ks[15], (DEPTH, N_GROUPS), 0.01),
        'w_router_expert': nrm(ks[16], (DEPTH, D_MODEL, N_EXPERTS), D_MODEL ** -0.5),
        'b_router_expert': nrm(ks[17], (DEPTH, N_EXPERTS), 0.01),
        'w_gate': nrm(ks[18], (DEPTH, N_EXPERTS, D_MODEL, D_EXPERT), D_MODEL ** -0.5),
        'w_up': nrm(ks[19], (DEPTH, N_EXPERTS, D_MODEL, D_EXPERT), D_MODEL ** -0.5),
        'w_down': nrm(ks[20], (DEPTH, N_EXPERTS, D_EXPERT, D_MODEL), D_EXPERT ** -0.5),
        'g_final': gain(ks[21], (D_MODEL,)),
    }


def reference(x_prompt, x_sample, cache_b_k, cache_b_v, g_mix, w_in, sgu_v_gain, sgu_w, sgu_b,
              rel_bias, g_out_a, g_out_b, w_out, g_ffn, w_router_group, b_router_group,
              w_router_expert, b_router_expert, w_gate, w_up, w_down, g_final):
    xp, xs = x_prompt, x_sample
    seq = xp.shape[1]
    keep_p = min(B_PAST_REACH, seq)
    kp_l, vp_l, ks_l, vs_l, av_l = [], [], [], [], []
    for l in range(DEPTH):
        moe_w = (w_router_group[l], b_router_group[l], w_router_expert[l], b_router_expert[l],
                 w_gate[l], w_up[l], w_down[l])
        u, va, q, k, vb = _project(xp, g_mix[l], w_in[l], sgu_v_gain[l])
        a = _sgu_prompt(u, va, sgu_w[l], sgu_b[l])
        o = _attn_prompt(q, k, vb, rel_bias[l])
        xp = xp + _merge(a, o, g_out_a[l], g_out_b[l], w_out[l])
        xp = xp + _hier_moe(_rmsnorm(xp, g_ffn[l]), *moe_w)
        kp_l.append(k[:, seq - keep_p:])
        vp_l.append(vb[:, seq - keep_p:])
        u, va, q, k, vb = _project(xs, g_mix[l], w_in[l], sgu_v_gain[l])
        a = _sgu_sample(u, va, sgu_w[l], sgu_b[l])
        o = _attn_sample_bias(q, k, vb, cache_b_k[l], cache_b_v[l], rel_bias[l])
        xs = xs + _merge(a, o, g_out_a[l], g_out_b[l], w_out[l])
        xs = xs + _hier_moe(_rmsnorm(xs, g_ffn[l]), *moe_w)
        ks_l.append(k)
        vs_l.append(vb)
        av_l.append(va.reshape(va.shape[0], va.shape[1], A_WIDTH))
    y_prompt = _rmsnorm(xp, g_final)
    y_sample = _rmsnorm(xs, g_final)
    return (y_prompt, y_sample, jnp.stack(kp_l), jnp.stack(vp_l), jnp.stack(ks_l), jnp.stack(vs_l), jnp.stack(av_l))
```

```python
import functools

import jax
import jax.numpy as jnp
from jax import lax
from jax.experimental import pallas as pl
from jax.experimental.pallas import tpu as pltpu

D_MODEL = 4096
SEQ = 8192
DEC_BATCH = 8
DEC_SEQ = 32
T_PROMPT = SEQ
T_SAMPLE = DEC_BATCH * DEC_SEQ
T_ALL = T_PROMPT + T_SAMPLE

CHUNK = 64
A_WIDTH = D_MODEL // 2
A_GROUPS = 4
A_GROUP_DIM = A_WIDTH // A_GROUPS
A_CHUNK = 128
B_WIDTH = D_MODEL - A_WIDTH
B_HEAD_DIM = 128
B_HEADS = B_WIDTH // B_HEAD_DIM
B_PAST_REACH = 8 * CHUNK
REL_MAX = 256
N_GROUPS = 8
EXPERTS_PER_GROUP = 8
N_EXPERTS = N_GROUPS * EXPERTS_PER_GROUP
D_EXPERT = D_MODEL // 4
EPS = 1e-6
NEG = -1e30

LANES = 128
VMEM_LIMIT_BYTES = 56 * 1024 * 1024
ROW_TILE = 256
MM_TM = 768
MM_TN = 1024
ATT_Q = 256
ATT_KB = B_PAST_REACH // ATT_Q + 1
SGU_CHUNKS_PER_STEP = 4
MOE_CH = 384
MOE_SUB = 128
MOE_TF = 256
MOE_NF = D_EXPERT // MOE_TF
MOE_SLAB = 1024
MOE_NC = (2 * T_ALL) // MOE_CH + N_EXPERTS
ROUTER_COLS = 128
HALF = D_MODEL // 2

assert T_ALL % MM_TM == 0 and T_PROMPT % ROW_TILE == 0 and T_SAMPLE == ROW_TILE


def _cparams(n_axes):
    return pltpu.CompilerParams(
        dimension_semantics=("arbitrary",) * n_axes,
        vmem_limit_bytes=VMEM_LIMIT_BYTES)


def _rms_scale(x):
    return lax.rsqrt(jnp.mean(x * x, axis=-1, keepdims=True) + EPS)


N_PROMPT_TILES = T_PROMPT // ROW_TILE
N_ROW_TILES = T_ALL // ROW_TILE


def _prenorm_kernel(xp_ref, xs_ref, g_ref, o_ref):
    i = pl.program_id(0)

    def norm(x):
        return (x * _rms_scale(x) * g_ref[...]).astype(o_ref.dtype)

    @pl.when(i < N_PROMPT_TILES)
    def _():
        o_ref[...] = norm(xp_ref[...])

    @pl.when(i >= N_PROMPT_TILES)
    def _():
        o_ref[...] = norm(xs_ref[...])


def _prompt_tile(i):
    return (jnp.minimum(i, N_PROMPT_TILES - 1), 0)


def _sample_tile(i):
    return (jnp.maximum(i - N_PROMPT_TILES, 0), 0)


def _prenorm(xp, xs, g):
    return pl.pallas_call(
        _prenorm_kernel,
        out_shape=jax.ShapeDtypeStruct((T_ALL, D_MODEL), jnp.bfloat16),
        grid=(N_ROW_TILES,),
        in_specs=[pl.BlockSpec((ROW_TILE, D_MODEL), _prompt_tile),
                  pl.BlockSpec((ROW_TILE, D_MODEL), _sample_tile),
                  pl.BlockSpec((1, D_MODEL), lambda i: (0, 0))],
        out_specs=pl.BlockSpec((ROW_TILE, D_MODEL), lambda i: (i, 0)),
        compiler_params=_cparams(1),
        name="prenorm",
    )(xp, xs, g)


def _uva_kernel(x_ref, w_ref, vg_ref, o_ref):
    j = pl.program_id(0)
    acc = jnp.dot(x_ref[...], w_ref[...], preferred_element_type=jnp.float32)
    g = jax.nn.gelu(acc)

    @pl.when(j < A_WIDTH // MM_TN)
    def _():
        o_ref[...] = g

    @pl.when(j >= A_WIDTH // MM_TN)
    def _():
        for k in range(MM_TN // A_GROUP_DIM):
            cols = slice(k * A_GROUP_DIM, (k + 1) * A_GROUP_DIM)
            gg = g[:, cols]
            o_ref[:, cols] = gg * _rms_scale(gg) * vg_ref[:, cols]


def _qkv_kernel(x_ref, w_ref, o16_ref, o32_ref):
    acc = jnp.dot(x_ref[...], w_ref[...], preferred_element_type=jnp.float32)
    o16_ref[...] = acc.astype(o16_ref.dtype)
    o32_ref[...] = acc


def _plain_mm_kernel(x_ref, w_ref, o_ref):
    o_ref[...] = jnp.dot(x_ref[...], w_ref[...], preferred_element_type=jnp.float32)


def _proj_uva(xn, w_in16, v_gain):
    n_va_tiles = A_WIDTH // MM_TN
    return pl.pallas_call(
        _uva_kernel,
        out_shape=jax.ShapeDtypeStruct((T_ALL, 2 * A_WIDTH), jnp.float32),
        grid=(2 * A_WIDTH // MM_TN, T_ALL // MM_TM),
        in_specs=[pl.BlockSpec((MM_TM, D_MODEL), lambda j, i: (i, 0)),
                  pl.BlockSpec((D_MODEL, MM_TN), lambda j, i: (0, j)),
                  pl.BlockSpec((1, MM_TN), lambda j, i: (0, jnp.maximum(j - n_va_tiles, 0)))],
        out_specs=pl.BlockSpec((MM_TM, MM_TN), lambda j, i: (i, j)),
        compiler_params=_cparams(2),
        name="proj_uva",
    )(xn, w_in16, v_gain)


def _proj_qkv(xn, w_in16):
    col0 = 2 * A_WIDTH // MM_TN
    n_cols = 3 * B_WIDTH
    shape = (T_ALL, n_cols)
    return pl.pallas_call(
        _qkv_kernel,
        out_shape=(jax.ShapeDtypeStruct(shape, jnp.bfloat16),
                   jax.ShapeDtypeStruct(shape, jnp.float32)),
        grid=(n_cols // MM_TN, T_ALL // MM_TM),
        in_specs=[pl.BlockSpec((MM_TM, D_MODEL), lambda j, i: (i, 0)),
                  pl.BlockSpec((D_MODEL, MM_TN), lambda j, i: (0, col0 + j))],
        out_specs=(pl.BlockSpec((MM_TM, MM_TN), lambda j, i: (i, j)),
                   pl.BlockSpec((MM_TM, MM_TN), lambda j, i: (i, j))),
        compiler_params=_cparams(2),
        name="proj_qkv",
    )(xn, w_in16)


def _proj_out(mix, w_out16):
    return pl.pallas_call(
        _plain_mm_kernel,
        out_shape=jax.ShapeDtypeStruct((T_ALL, D_MODEL), jnp.float32),
        grid=(D_MODEL // MM_TN, T_ALL // MM_TM),
        in_specs=[pl.BlockSpec((MM_TM, D_MODEL), lambda j, i: (i, 0)),
                  pl.BlockSpec((D_MODEL, MM_TN), lambda j, i: (0, j))],
        out_specs=pl.BlockSpec((MM_TM, MM_TN), lambda j, i: (i, j)),
        compiler_params=_cparams(2),
        name="proj_out",
    )(mix, w_out16)


def _sgu_kernel(u_ref, v_ref, w_ref, b_ref, ga_ref, *rest, chunk_len, n_chunks):
    o_ref = rest[-1]
    L = chunk_len
    row = lax.broadcasted_iota(jnp.int32, (L, L), 0)
    col = lax.broadcasted_iota(jnp.int32, (L, L), 1)
    w16 = [jnp.where(col <= row, w_ref[g], 0.0).astype(jnp.bfloat16)
           for g in range(A_GROUPS)]
    for c in range(n_chunks):
        rows = slice(c * L, (c + 1) * L)
        parts = []
        ss = jnp.zeros((L, 1), jnp.float32)
        for g in range(A_GROUPS):
            cols = slice(g * A_GROUP_DIM, (g + 1) * A_GROUP_DIM)
            f = jnp.dot(w16[g], v_ref[rows, cols].astype(jnp.bfloat16),
                        preferred_element_type=jnp.float32) + b_ref[:, g:g + 1]
            a = u_ref[rows, cols] * f
            parts.append(a)
            ss = ss + jnp.sum(a * a, axis=-1, keepdims=True)
        r = lax.rsqrt(ss * (1.0 / A_WIDTH) + EPS)
        for g in range(A_GROUPS):
            cols = slice(g * A_GROUP_DIM, (g + 1) * A_GROUP_DIM)
            o_ref[rows, cols] = (parts[g] * r * ga_ref[:, cols]).astype(o_ref.dtype)


def _sgu(uva, w_s, b_s_t, g_a, mix, *, chunk_len, n_chunks, row0, n_steps):
    rows = chunk_len * n_chunks
    rb0 = row0 // rows
    kernel = functools.partial(_sgu_kernel, chunk_len=chunk_len, n_chunks=n_chunks)
    return pl.pallas_call(
        kernel,
        out_shape=jax.ShapeDtypeStruct((T_ALL, D_MODEL), jnp.bfloat16),
        grid=(n_steps,),
        in_specs=[pl.BlockSpec((rows, A_WIDTH), lambda i: (rb0 + i, 0)),
                  pl.BlockSpec((rows, A_WIDTH), lambda i: (rb0 + i, 1)),
                  pl.BlockSpec((A_GROUPS, chunk_len, chunk_len), lambda i: (0, 0, 0)),
                  pl.BlockSpec((chunk_len, A_GROUPS), lambda i: (0, 0)),
                  pl.BlockSpec((1, A_WIDTH), lambda i: (0, 0)),
                  pl.BlockSpec(memory_space=pl.ANY)],
        out_specs=pl.BlockSpec((rows, A_WIDTH), lambda i: (rb0 + i, 0)),
        input_output_aliases={5: 0},
        compiler_params=_cparams(1),
        name="sgu_%d" % chunk_len,
    )(uva, uva, w_s, b_s_t, g_a, mix)


_SCALE = B_HEAD_DIM ** -0.5
_NT = (((1,), (1,)), ((), ()))


def _attn_prompt_kernel(q_ref, *rest):
    k_refs = rest[:ATT_KB]
    v_refs = rest[ATT_KB:2 * ATT_KB]
    bias_ref, gb_ref, _, o_ref, acc_ref = rest[2 * ATT_KB:]
    i = pl.program_id(0)
    for h in range(B_HEADS):
        hs = slice(h * B_HEAD_DIM, (h + 1) * B_HEAD_DIM)
        q = q_ref[:, hs]
        s = []
        for b in range(ATT_KB):
            sb = lax.dot_general(q, k_refs[b][:, hs], _NT,
                                 preferred_element_type=jnp.float32)
            sb = sb * _SCALE + bias_ref[h, :, b * ATT_Q:(b + 1) * ATT_Q]
            if b < ATT_KB - 1:
                sb = jnp.where(i + b >= ATT_KB - 1, sb, NEG)
            s.append(sb)
        m = functools.reduce(jnp.maximum,
                             [jnp.max(sb, axis=-1, keepdims=True) for sb in s])
        p = [jnp.exp(sb - m) for sb in s]
        l = functools.reduce(lambda a, b: a + b,
                             [jnp.sum(pb, axis=-1, keepdims=True) for pb in p])
        o = functools.reduce(lambda a, b: a + b, [
            jnp.dot(p[b].astype(jnp.bfloat16), v_refs[b][:, hs],
                    preferred_element_type=jnp.float32) for b in range(ATT_KB)])
        acc_ref[:, hs] = o / l
    o = acc_ref[...]
    o_ref[...] = (o * _rms_scale(o) * gb_ref[...]).astype(o_ref.dtype)


def _attn_prompt(qkv16, bias, g_b, mix):
    n_blocks = T_PROMPT // ATT_Q

    def kv_spec(b, col):
        return pl.BlockSpec(
            (ATT_Q, B_WIDTH),
            lambda i: (jnp.maximum(i + b - (ATT_KB - 1), 0), col))

    in_specs = ([pl.BlockSpec((ATT_Q, B_WIDTH), lambda i: (i, 0))]
                + [kv_spec(b, 1) for b in range(ATT_KB)]
                + [kv_spec(b, 2) for b in range(ATT_KB)]
                + [pl.BlockSpec((B_HEADS, ATT_Q, ATT_KB * ATT_Q), lambda i: (0, 0, 0)),
                   pl.BlockSpec((1, B_WIDTH), lambda i: (0, 0)),
                   pl.BlockSpec(memory_space=pl.ANY)])
    n_in = len(in_specs)
    return pl.pallas_call(
        _attn_prompt_kernel,
        out_shape=jax.ShapeDtypeStruct((T_ALL, D_MODEL), jnp.bfloat16),
        grid=(n_blocks,),
        in_specs=in_specs,
        out_specs=pl.BlockSpec((ATT_Q, B_WIDTH), lambda i: (i, 1)),
        scratch_shapes=[pltpu.VMEM((ATT_Q, B_WIDTH), jnp.float32)],
        input_output_aliases={n_in - 1: 0},
        compiler_params=_cparams(1),
        name="attn_prompt",
    )(qkv16, *([qkv16] * (2 * ATT_KB)), bias, g_b, mix)


def _attn_sample_kernel(q_ref, kn_ref, vn_ref, kc_ref, vc_ref, bias_ref, gb_ref, _,
                        o_ref, acc_ref):
    n_cache = kc_ref.shape[0]
    for h in range(B_HEADS):
        hs = slice(h * B_HEAD_DIM, (h + 1) * B_HEAD_DIM)
        q = q_ref[:, hs]
        kc = kc_ref[:, hs].astype(jnp.bfloat16)
        vc = vc_ref[:, hs].astype(jnp.bfloat16)
        sc = lax.dot_general(q, kc, _NT, preferred_element_type=jnp.float32)
        sc = sc * _SCALE + bias_ref[h, :, :n_cache]
        sn = lax.dot_general(q, kn_ref[:, hs], _NT, preferred_element_type=jnp.float32)
        sn = sn * _SCALE + bias_ref[h, :, n_cache:]
        m = jnp.maximum(jnp.max(sc, axis=-1, keepdims=True),
                        jnp.max(sn, axis=-1, keepdims=True))
        pc = jnp.exp(sc - m)
        pn = jnp.exp(sn - m)
        l = jnp.sum(pc, axis=-1, keepdims=True) + jnp.sum(pn, axis=-1, keepdims=True)
        o = (jnp.dot(pc.astype(jnp.bfloat16), vc, preferred_element_type=jnp.float32)
             + jnp.dot(pn.astype(jnp.bfloat16), vn_ref[:, hs],
                       preferred_element_type=jnp.float32))
        acc_ref[:, hs] = o / l
    o = acc_ref[...]
    o_ref[...] = (o * _rms_scale(o) * gb_ref[...]).astype(o_ref.dtype)


def _attn_sample(qkv16, cache_k, cache_v, bias, g_b, mix):
    n_cache = cache_k.shape[1]
    rb0 = T_PROMPT // DEC_SEQ
    new_spec = lambda col: pl.BlockSpec((DEC_SEQ, B_WIDTH), lambda b: (rb0 + b, col))
    cache_spec = pl.BlockSpec((None, n_cache, B_WIDTH), lambda b: (b, 0, 0))
    return pl.pallas_call(
        _attn_sample_kernel,
        out_shape=jax.ShapeDtypeStruct((T_ALL, D_MODEL), jnp.bfloat16),
        grid=(DEC_BATCH,),
        in_specs=[new_spec(0), new_spec(1), new_spec(2), cache_spec, cache_spec,
                  pl.BlockSpec((B_HEADS, DEC_SEQ, n_cache + DEC_SEQ), lambda b: (0, 0, 0)),
                  pl.BlockSpec((1, B_WIDTH), lambda b: (0, 0)),
                  pl.BlockSpec(memory_space=pl.ANY)],
        out_specs=pl.BlockSpec((DEC_SEQ, B_WIDTH), lambda b: (rb0 + b, 1)),
        scratch_shapes=[pltpu.VMEM((DEC_SEQ, B_WIDTH), jnp.float32)],
        input_output_aliases={7: 0},
        compiler_params=_cparams(1),
        name="attn_sample",
    )(qkv16, qkv16, qkv16, cache_k, cache_v, bias, g_b, mix)


def _rel_index(n_q, n_k, offset):
    d = jnp.arange(n_q)[:, None] + offset - jnp.arange(n_k)[None, :]
    return jnp.clip(d, -(CHUNK - 1), REL_MAX) + (CHUNK - 1)


def _prompt_bias(table):
    n_k = ATT_KB * ATT_Q
    bias = table[:, _rel_index(ATT_Q, n_k, B_PAST_REACH)].astype(jnp.float32)
    qc = jnp.arange(ATT_Q)[:, None] // CHUNK
    kc = jnp.arange(n_k)[None, :] // CHUNK
    band = (kc >= qc) & (kc <= qc + B_PAST_REACH // CHUNK)
    return jnp.where(band[None], bias, NEG)


def _split_bf16(x):
    hi = x.astype(jnp.bfloat16)
    lo = (x - hi.astype(jnp.float32)).astype(jnp.bfloat16)
    return hi, lo


def _bits(x):
    return lax.bitcast_convert_type(x, jnp.uint32)


def _router_kernel(xp_ref, xs_ref, hp_ref, g_ref, wr_ref, br_ref,
                   h_ref, xt_ref, eid_ref, gate_ref):
    i = pl.program_id(0)

    @pl.when(i < N_PROMPT_TILES)
    def _():
        h_ref[...] = xp_ref[...] + hp_ref[...]

    @pl.when(i >= N_PROMPT_TILES)
    def _():
        h_ref[...] = xs_ref[...] + hp_ref[...]

    h = h_ref[...]
    xt = h * _rms_scale(h) * g_ref[...]
    a = xt[:, :HALF].astype(jnp.bfloat16).astype(jnp.float32)
    b = xt[:, HALF:].astype(jnp.bfloat16).astype(jnp.float32)
    xt_ref[...] = _bits(a) | (_bits(b) >> 16)

    xh, xl = _split_bf16(xt)
    wh, wl = _split_bf16(wr_ref[...])
    dot = functools.partial(jnp.dot, preferred_element_type=jnp.float32)
    logits = dot(xh, wh) + (dot(xl, wh) + dot(xh, wl)) + br_ref[...]

    lane = lax.broadcasted_iota(jnp.int32, logits.shape, 1)
    big = jnp.int32(ROUTER_COLS)

    def masked_max(x, mask):
        return jnp.max(jnp.where(mask, x, NEG), axis=-1, keepdims=True)

    def first_lane(mask):
        return jnp.min(jnp.where(mask, lane, big), axis=-1, keepdims=True)

    gmask = lane < N_GROUPS
    gmax = masked_max(logits, gmask)
    gsum = jnp.sum(jnp.where(gmask, jnp.exp(logits - gmax), 0.0), axis=-1, keepdims=True)
    g_sel = first_lane(gmask & (logits == gmax))
    pg_top = 1.0 / gsum

    e_lo = N_GROUPS + g_sel * EXPERTS_PER_GROUP
    emask = (lane >= e_lo) & (lane < e_lo + EXPERTS_PER_GROUP)
    emax = masked_max(logits, emask)
    ex = jnp.where(emask, jnp.exp(logits - emax), 0.0)
    pe = ex / jnp.sum(ex, axis=-1, keepdims=True)
    p1 = jnp.max(pe, axis=-1, keepdims=True)
    l1 = first_lane(emask & (pe == p1))
    rest = emask & (lane != l1)
    p2 = jnp.max(jnp.where(rest, pe, -1.0), axis=-1, keepdims=True)
    l2 = first_lane(rest & (pe == p2))
    denom = p1 + p2
    g1 = pg_top * p1 / denom
    g2 = pg_top * p2 / denom
    eid_ref[...] = jnp.where(lane == 0, l1 - N_GROUPS,
                             jnp.where(lane == 1, l2 - N_GROUPS, 0))
    gate_ref[...] = jnp.where(lane == 0, g1, jnp.where(lane == 1, g2, 0.0))


def _router(xp, xs, hproj, g_ffn, w_router, b_router):
    tile = lambda cols: pl.BlockSpec((ROW_TILE, cols), lambda i: (i, 0))
    return pl.pallas_call(
        _router_kernel,
        out_shape=(jax.ShapeDtypeStruct((T_ALL, D_MODEL), jnp.float32),
                   jax.ShapeDtypeStruct((T_ALL, HALF), jnp.uint32),
                   jax.ShapeDtypeStruct((T_ALL, ROUTER_COLS), jnp.int32),
                   jax.ShapeDtypeStruct((T_ALL, ROUTER_COLS), jnp.float32)),
        grid=(N_ROW_TILES,),
        in_specs=[pl.BlockSpec((ROW_TILE, D_MODEL), _prompt_tile),
                  pl.BlockSpec((ROW_TILE, D_MODEL), _sample_tile),
                  tile(D_MODEL),
                  pl.BlockSpec((1, D_MODEL), lambda i: (0, 0)),
                  pl.BlockSpec((D_MODEL, ROUTER_COLS), lambda i: (0, 0)),
                  pl.BlockSpec((1, ROUTER_COLS), lambda i: (0, 0))],
        out_specs=(tile(D_MODEL), tile(HALF), tile(ROUTER_COLS), tile(ROUTER_COLS)),
        compiler_params=_cparams(1),
        name="router",
    )(xp, xs, hproj, g_ffn, w_router, b_router)


def _gather_kernel(nv_ref, idx_ref, src_ref, dst_ref, sem, *, rows_per_step, n_steps):
    s = pl.program_id(0)
    nv = nv_ref[0]
    R = rows_per_step

    def row_copy(src_row, dst_row):
        return pltpu.make_async_copy(src_ref.at[pl.ds(src_row, 1)],
                                     dst_ref.at[pl.ds(dst_row, 1)], sem)

    def wait_step():
        def body(r, c):
            row_copy(0, 0).wait()
            return c
        lax.fori_loop(0, R, body, 0)

    @pl.when(s < nv)
    def _():
        def body(r, c):
            row_copy(idx_ref[0, 0, r], s * R + r).start()
            return c
        lax.fori_loop(0, R, body, 0, unroll=8)

    @pl.when((s >= 1) & (s - 1 < nv))
    def _():
        wait_step()

    @pl.when((s == n_steps - 1) & (s < nv))
    def _():
        wait_step()


def _gather_rows(src, idx, n_valid_steps):
    n_steps, R = idx.shape
    kernel = functools.partial(_gather_kernel, rows_per_step=R, n_steps=n_steps)
    grid_spec = pltpu.PrefetchScalarGridSpec(
        num_scalar_prefetch=1,
        grid=(n_steps,),
        in_specs=[pl.BlockSpec((1, 1, R), lambda s, nv: (s, 0, 0), memory_space=pltpu.SMEM),
                  pl.BlockSpec(memory_space=pl.ANY)],
        out_specs=pl.BlockSpec(memory_space=pl.ANY),
        scratch_shapes=[pltpu.SemaphoreType.DMA(())],
    )
    return pl.pallas_call(
        kernel,
        out_shape=jax.ShapeDtypeStruct((n_steps * R, src.shape[1]), src.dtype),
        grid_spec=grid_spec,
        compiler_params=_cparams(1),
        name="gather_rows_%d" % src.shape[1],
    )(n_valid_steps, idx.reshape(n_steps, 1, R), src)


def _unpack_rows(x):
    hi = lax.bitcast_convert_type(x & jnp.uint32(0xFFFF0000), jnp.float32)
    lo = lax.bitcast_convert_type(x << 16, jnp.float32)
    return jnp.concatenate([hi.astype(jnp.bfloat16), lo.astype(jnp.bfloat16)], axis=-1)


def _expert_kernel(ce_ref, cr_ref, nv_ref, x_ref, wg_ref, wu_ref, wd_ref, o_ref):
    c = pl.program_id(0)
    j = pl.program_id(1)
    rows = cr_ref[c]
    n_sub_max = MOE_CH // MOE_SUB

    def dot16(a, w):
        return jnp.dot(a, w.astype(jnp.bfloat16), preferred_element_type=jnp.float32)

    @pl.when(c < nv_ref[0])
    def _():
        for n_sub in range(1, n_sub_max + 1):
            lo, hi = (n_sub - 1) * MOE_SUB, n_sub * MOE_SUB
            cond = (rows > lo) & (rows <= hi) if n_sub < n_sub_max else rows > lo

            @pl.when(cond)
            def _():
                m = n_sub * MOE_SUB
                x = _unpack_rows(x_ref[:m, :])
                hg = jnp.zeros((m, MOE_TF), jnp.float32)
                hu = jnp.zeros((m, MOE_TF), jnp.float32)
                for s in range(D_MODEL // MOE_SLAB):
                    ks = slice(s * MOE_SLAB, (s + 1) * MOE_SLAB)
                    hg = hg + dot16(x[:, ks], wg_ref[ks, :])
                    hu = hu + dot16(x[:, ks], wu_ref[ks, :])
                hh = (jax.nn.silu(hg) * hu).astype(jnp.bfloat16)
                for s in range(D_MODEL // MOE_SLAB):
                    ns = slice(s * MOE_SLAB, (s + 1) * MOE_SLAB)
                    y = dot16(hh, wd_ref[:, ns])

                    @pl.when(j == 0)
                    def _():
                        o_ref[:m, ns] = y

                    @pl.when(j > 0)
                    def _():
                        o_ref[:m, ns] += y

                if m < MOE_CH:
                    @pl.when(j == 0)
                    def _():
                        o_ref[m:, :] = jnp.zeros((MOE_CH - m, D_MODEL), jnp.float32)


def _experts(chunk_expert, chunk_rows, n_chunks, xs, w_gate, w_up, w_down):
    def chunk_of(c, nv):
        return jnp.minimum(c, nv[0] - 1)

    def ftile_of(c, j, nv):
        return jnp.where(c < nv[0], j, MOE_NF - 1)

    grid_spec = pltpu.PrefetchScalarGridSpec(
        num_scalar_prefetch=3,
        grid=(MOE_NC, MOE_NF),
        in_specs=[
            pl.BlockSpec((MOE_CH, HALF), lambda c, j, ce, cr, nv: (chunk_of(c, nv), 0)),
            pl.BlockSpec((None, D_MODEL, MOE_TF),
                         lambda c, j, ce, cr, nv: (ce[chunk_of(c, nv)], 0, ftile_of(c, j, nv))),
            pl.BlockSpec((None, D_MODEL, MOE_TF),
                         lambda c, j, ce, cr, nv: (ce[chunk_of(c, nv)], 0, ftile_of(c, j, nv))),
            pl.BlockSpec((None, MOE_TF, D_MODEL),
                         lambda c, j, ce, cr, nv: (ce[chunk_of(c, nv)], ftile_of(c, j, nv), 0)),
        ],
        out_specs=pl.BlockSpec((MOE_CH, D_MODEL), lambda c, j, ce, cr, nv: (chunk_of(c, nv), 0)),
    )
    return pl.pallas_call(
        _expert_kernel,
        out_shape=jax.ShapeDtypeStruct((MOE_NC * MOE_CH, D_MODEL), jnp.float32),
        grid_spec=grid_spec,
        compiler_params=_cparams(2),
        name="experts",
    )(chunk_expert, chunk_rows, n_chunks, xs, w_gate, w_up, w_down)


def _moe_plan(eid):
    flat_e = eid.reshape(-1)
    n = flat_e.shape[0]
    order = jnp.argsort(flat_e, stable=True).astype(jnp.int32)
    se = flat_e[order]
    counts = jnp.zeros((N_EXPERTS,), jnp.int32).at[flat_e].add(1)
    start = jnp.cumsum(counts) - counts
    nchunk = (counts + MOE_CH - 1) // MOE_CH
    cend = jnp.cumsum(nchunk)
    cstart = cend - nchunk
    dest = cstart[se] * MOE_CH + (jnp.arange(n, dtype=jnp.int32) - start[se])
    row_tok = jnp.zeros((MOE_NC * MOE_CH,), jnp.int32).at[dest].set(order // 2)
    pos = jnp.zeros((n,), jnp.int32).at[order].set(dest)
    cidx = jnp.arange(MOE_NC, dtype=jnp.int32)
    chunk_expert = jnp.clip(jnp.searchsorted(cend, cidx, side="right"),
                            0, N_EXPERTS - 1).astype(jnp.int32)
    chunk_rows = jnp.clip(counts[chunk_expert] - (cidx - cstart[chunk_expert]) * MOE_CH,
                          0, MOE_CH)
    n_chunks = cend[-1:]
    chunk_rows = jnp.where(cidx < n_chunks[0], chunk_rows, 0).astype(jnp.int32)
    return row_tok, pos, chunk_expert, chunk_rows, n_chunks.astype(jnp.int32)


def _combine_kernel(h_ref, y2_ref, gate_ref, g_ref, op_ref, os_ref):
    i = pl.program_id(0)
    x = (h_ref[...] + gate_ref[:, 0:1] * y2_ref[:, :D_MODEL]
         + gate_ref[:, 1:2] * y2_ref[:, D_MODEL:])
    y = x * _rms_scale(x) * g_ref[...]

    @pl.when(i < N_PROMPT_TILES)
    def _():
        op_ref[...] = y

    @pl.when(i >= N_PROMPT_TILES)
    def _():
        os_ref[...] = y


def _combine(h, y2, gates, g_final):
    return pl.pallas_call(
        _combine_kernel,
        out_shape=(jax.ShapeDtypeStruct((T_PROMPT, D_MODEL), jnp.float32),
                   jax.ShapeDtypeStruct((T_SAMPLE, D_MODEL), jnp.float32)),
        grid=(N_ROW_TILES,),
        in_specs=[pl.BlockSpec((ROW_TILE, D_MODEL), lambda i: (i, 0)),
                  pl.BlockSpec((ROW_TILE, 2 * D_MODEL), lambda i: (i, 0)),
                  pl.BlockSpec((ROW_TILE, ROUTER_COLS), lambda i: (i, 0)),
                  pl.BlockSpec((1, D_MODEL), lambda i: (0, 0))],
        out_specs=(pl.BlockSpec((ROW_TILE, D_MODEL), _prompt_tile),
                   pl.BlockSpec((ROW_TILE, D_MODEL), _sample_tile)),
        compiler_params=_cparams(1),
        name="combine",
    )(h, y2, gates, g_final)


def kernel(x_prompt, x_sample, cache_b_k, cache_b_v, g_mix, w_in, sgu_v_gain, sgu_w, sgu_b,
           rel_bias, g_out_a, g_out_b, w_out, g_ffn, w_router_group, b_router_group,
           w_router_expert, b_router_expert, w_gate, w_up, w_down, g_final):
    assert w_in.shape[0] == 1, "one layer"
    xp = x_prompt.reshape(T_PROMPT, D_MODEL)
    xs = x_sample.reshape(T_SAMPLE, D_MODEL)
    row = lambda v: v.reshape(1, -1)

    xn = _prenorm(xp, xs, row(g_mix[0]))
    w_in16 = w_in[0].astype(jnp.bfloat16)
    uva = _proj_uva(xn, w_in16, row(sgu_v_gain[0]))
    qkv16, qkv32 = _proj_qkv(xn, w_in16)

    w_s, b_s = sgu_w[0], sgu_b[0]
    mix = _sgu(uva, w_s, b_s.T, row(g_out_a[0]), xn,
               chunk_len=A_CHUNK, n_chunks=SGU_CHUNKS_PER_STEP, row0=0,
               n_steps=T_PROMPT // (A_CHUNK * SGU_CHUNKS_PER_STEP))
    mix = _sgu(uva, w_s[:, :DEC_SEQ, :DEC_SEQ], b_s[:, :DEC_SEQ].T, row(g_out_a[0]), mix,
               chunk_len=DEC_SEQ, n_chunks=DEC_BATCH, row0=T_PROMPT, n_steps=1)

    table = rel_bias[0]
    n_cache = cache_b_k.shape[2]
    mix = _attn_prompt(qkv16, _prompt_bias(table), row(g_out_b[0]), mix)
    bias_s = table[:, _rel_index(DEC_SEQ, n_cache + DEC_SEQ, n_cache)].astype(jnp.float32)
    mix = _attn_sample(qkv16, cache_b_k[0].reshape(DEC_BATCH, n_cache, B_WIDTH),
                       cache_b_v[0].reshape(DEC_BATCH, n_cache, B_WIDTH),
                       bias_s, row(g_out_b[0]), mix)

    hproj = _proj_out(mix, w_out[0].astype(jnp.bfloat16))

    pad = ROUTER_COLS - N_GROUPS - N_EXPERTS
    w_router = jnp.pad(jnp.concatenate([w_router_group[0], w_router_expert[0]], axis=1),
                       ((0, 0), (0, pad)))
    b_router = jnp.pad(jnp.concatenate([b_router_group[0], b_router_expert[0]]), (0, pad))
    h, xt, eid, gates = _router(xp, xs, hproj, row(g_ffn[0]), w_router, row(b_router))

    row_tok, pos, chunk_expert, chunk_rows, n_chunks = _moe_plan(eid[:, :2])
    x_sorted = _gather_rows(xt, row_tok.reshape(MOE_NC, MOE_CH), n_chunks)
    y_sorted = _experts(chunk_expert, chunk_rows, n_chunks, x_sorted,
                        w_gate[0], w_up[0], w_down[0])
    n_tiles = jnp.full((1,), N_ROW_TILES, jnp.int32)
    y2 = _gather_rows(y_sorted, pos.reshape(N_ROW_TILES, 2 * ROW_TILE), n_tiles)
    y_prompt, y_sample = _combine(h, y2.reshape(T_ALL, 2 * D_MODEL), gates, row(g_final))

    keep = min(B_PAST_REACH, SEQ)
    k32 = qkv32[:, B_WIDTH:2 * B_WIDTH]
    v32 = qkv32[:, 2 * B_WIDTH:]
    heads = lambda t, lead: t.reshape(*lead, B_HEADS, B_HEAD_DIM)
    return (y_prompt.reshape(1, SEQ, D_MODEL),
            y_sample.reshape(DEC_BATCH, DEC_SEQ, D_MODEL),
            heads(k32[T_PROMPT - keep:T_PROMPT], (1, 1, keep)),
            heads(v32[T_PROMPT - keep:T_PROMPT], (1, 1, keep)),
            heads(k32[T_PROMPT:], (1, DEC_BATCH, DEC_SEQ)),
            heads(v32[T_PROMPT:], (1, DEC_BATCH, DEC_SEQ)),
            uva[T_PROMPT:, A_WIDTH:].reshape(1, DEC_BATCH, DEC_SEQ, A_WIDTH))
```

```python
import functools

import numpy as np
import jax
import jax.numpy as jnp
from jax import lax
from jax.experimental import pallas as pl
from jax.experimental.pallas import tpu as pltpu

D_MODEL = 4096
SEQ = 8192
DEC_BATCH = 8
DEC_SEQ = 32
T_PROMPT = SEQ
T_SAMPLE = DEC_BATCH * DEC_SEQ
T_ALL = T_PROMPT + T_SAMPLE

CHUNK = 64
A_WIDTH = D_MODEL // 2
A_GROUPS = 4
A_GROUP_DIM = A_WIDTH // A_GROUPS
A_CHUNK = 128
B_WIDTH = D_MODEL - A_WIDTH
B_HEAD_DIM = 128
B_HEADS = B_WIDTH // B_HEAD_DIM
B_PAST_REACH = 8 * CHUNK
REL_MAX = 256
N_GROUPS = 8
EXPERTS_PER_GROUP = 8
N_EXPERTS = N_GROUPS * EXPERTS_PER_GROUP
D_EXPERT = D_MODEL // 4
EPS = 1e-6
NEG = -1e30

LANES = 128
VMEM_LIMIT_BYTES = 56 * 1024 * 1024
ROW_TILE = 256
COMBINE_TILE = 128
MM_TM = 768
MM_TN = 1024
ATT_Q = 256
ATT_KB = B_PAST_REACH // ATT_Q + 1
SGU_CHUNKS_PER_STEP = 4
MOE_CH = 384
MOE_SUB = 128
MOE_TF = 256
MOE_NF = D_EXPERT // MOE_TF
MOE_SLAB = 1024
MOE_NC = (2 * T_ALL) // MOE_CH + N_EXPERTS
ROUTER_COLS = 128
HALF = D_MODEL // 2

assert T_ALL % MM_TM == 0 and T_PROMPT % ROW_TILE == 0 and T_SAMPLE == ROW_TILE


def _cparams(n_axes):
    return pltpu.CompilerParams(
        dimension_semantics=("arbitrary",) * n_axes,
        vmem_limit_bytes=VMEM_LIMIT_BYTES)


def _rms_scale(x):
    return lax.rsqrt(jnp.mean(x * x, axis=-1, keepdims=True) + EPS)


N_PROMPT_TILES = T_PROMPT // ROW_TILE
N_ROW_TILES = T_ALL // ROW_TILE


def _prenorm_kernel(xp_ref, xs_ref, g_ref, o_ref):
    i = pl.program_id(0)

    def norm(x):
        return (x * _rms_scale(x) * g_ref[...]).astype(o_ref.dtype)

    @pl.when(i < N_PROMPT_TILES)
    def _():
        o_ref[...] = norm(xp_ref[...])

    @pl.when(i >= N_PROMPT_TILES)
    def _():
        o_ref[...] = norm(xs_ref[...])


def _prompt_tile(i):
    return (jnp.minimum(i, N_PROMPT_TILES - 1), 0)


def _sample_tile(i):
    return (jnp.maximum(i - N_PROMPT_TILES, 0), 0)


def _prenorm(xp, xs, g):
    return pl.pallas_call(
        _prenorm_kernel,
        out_shape=jax.ShapeDtypeStruct((T_ALL, D_MODEL), jnp.bfloat16),
        grid=(N_ROW_TILES,),
        in_specs=[pl.BlockSpec((ROW_TILE, D_MODEL), _prompt_tile),
                  pl.BlockSpec((ROW_TILE, D_MODEL), _sample_tile),
                  pl.BlockSpec((1, D_MODEL), lambda i: (0, 0))],
        out_specs=pl.BlockSpec((ROW_TILE, D_MODEL), lambda i: (i, 0)),
        compiler_params=_cparams(1),
        name="prenorm",
    )(xp, xs, g)


def _uva_kernel(x_ref, w_ref, vg_ref, o_ref):
    j = pl.program_id(0)
    acc = jnp.dot(x_ref[...], w_ref[...], preferred_element_type=jnp.float32)
    g = jax.nn.gelu(acc)

    @pl.when(j < A_WIDTH // MM_TN)
    def _():
        o_ref[...] = g

    @pl.when(j >= A_WIDTH // MM_TN)
    def _():
        for k in range(MM_TN // A_GROUP_DIM):
            cols = slice(k * A_GROUP_DIM, (k + 1) * A_GROUP_DIM)
            gg = g[:, cols]
            o_ref[:, cols] = gg * _rms_scale(gg) * vg_ref[:, cols]


def _qkv_kernel(x_ref, w_ref, o16_ref, o32_ref):
    acc = jnp.dot(x_ref[...], w_ref[...], preferred_element_type=jnp.float32)
    o16_ref[...] = acc.astype(o16_ref.dtype)
    o32_ref[...] = acc


def _plain_mm_kernel(x_ref, w_ref, o_ref):
    o_ref[...] = jnp.dot(x_ref[...], w_ref[...], preferred_element_type=jnp.float32)


def _proj_uva(xn, w_in16, v_gain):
    n_va_tiles = A_WIDTH // MM_TN
    return pl.pallas_call(
        _uva_kernel,
        out_shape=jax.ShapeDtypeStruct((T_ALL, 2 * A_WIDTH), jnp.float32),
        grid=(2 * A_WIDTH // MM_TN, T_ALL // MM_TM),
        in_specs=[pl.BlockSpec((MM_TM, D_MODEL), lambda j, i: (i, 0)),
                  pl.BlockSpec((D_MODEL, MM_TN), lambda j, i: (0, j)),
                  pl.BlockSpec((1, MM_TN), lambda j, i: (0, jnp.maximum(j - n_va_tiles, 0)))],
        out_specs=pl.BlockSpec((MM_TM, MM_TN), lambda j, i: (i, j)),
        compiler_params=_cparams(2),
        name="proj_uva",
    )(xn, w_in16, v_gain)


def _proj_qkv(xn, w_in16):
    col0 = 2 * A_WIDTH // MM_TN
    n_cols = 3 * B_WIDTH
    shape = (T_ALL, n_cols)
    return pl.pallas_call(
        _qkv_kernel,
        out_shape=(jax.ShapeDtypeStruct(shape, jnp.bfloat16),
                   jax.ShapeDtypeStruct(shape, jnp.float32)),
        grid=(n_cols // MM_TN, T_ALL // MM_TM),
        in_specs=[pl.BlockSpec((MM_TM, D_MODEL), lambda j, i: (i, 0)),
                  pl.BlockSpec((D_MODEL, MM_TN), lambda j, i: (0, col0 + j))],
        out_specs=(pl.BlockSpec((MM_TM, MM_TN), lambda j, i: (i, j)),
                   pl.BlockSpec((MM_TM, MM_TN), lambda j, i: (i, j))),
        compiler_params=_cparams(2),
        name="proj_qkv",
    )(xn, w_in16)


def _proj_out(mix, w_out16):
    return pl.pallas_call(
        _plain_mm_kernel,
        out_shape=jax.ShapeDtypeStruct((T_ALL, D_MODEL), jnp.float32),
        grid=(D_MODEL // MM_TN, T_ALL // MM_TM),
        in_specs=[pl.BlockSpec((MM_TM, D_MODEL), lambda j, i: (i, 0)),
                  pl.BlockSpec((D_MODEL, MM_TN), lambda j, i: (0, j))],
        out_specs=pl.BlockSpec((MM_TM, MM_TN), lambda j, i: (i, j)),
        compiler_params=_cparams(2),
        name="proj_out",
    )(mix, w_out16)


def _sgu_kernel(u_ref, v_ref, w_ref, b_ref, ga_ref, *rest, chunk_len, n_chunks):
    o_ref = rest[-1]
    L = chunk_len
    row = lax.broadcasted_iota(jnp.int32, (L, L), 0)
    col = lax.broadcasted_iota(jnp.int32, (L, L), 1)
    w16 = [jnp.where(col <= row, w_ref[g], 0.0).astype(jnp.bfloat16)
           for g in range(A_GROUPS)]
    for c in range(n_chunks):
        rows = slice(c * L, (c + 1) * L)
        parts = []
        ss = jnp.zeros((L, 1), jnp.float32)
        for g in range(A_GROUPS):
            cols = slice(g * A_GROUP_DIM, (g + 1) * A_GROUP_DIM)
            f = jnp.dot(w16[g], v_ref[rows, cols].astype(jnp.bfloat16),
                        preferred_element_type=jnp.float32) + b_ref[:, g:g + 1]
            a = u_ref[rows, cols] * f
            parts.append(a)
            ss = ss + jnp.sum(a * a, axis=-1, keepdims=True)
        r = lax.rsqrt(ss * (1.0 / A_WIDTH) + EPS)
        for g in range(A_GROUPS):
            cols = slice(g * A_GROUP_DIM, (g + 1) * A_GROUP_DIM)
            o_ref[rows, cols] = (parts[g] * r * ga_ref[:, cols]).astype(o_ref.dtype)


def _sgu(uva, w_s, b_s_t, g_a, mix, *, chunk_len, n_chunks, row0, n_steps):
    rows = chunk_len * n_chunks
    rb0 = row0 // rows
    kernel = functools.partial(_sgu_kernel, chunk_len=chunk_len, n_chunks=n_chunks)
    return pl.pallas_call(
        kernel,
        out_shape=jax.ShapeDtypeStruct((T_ALL, D_MODEL), jnp.bfloat16),
        grid=(n_steps,),
        in_specs=[pl.BlockSpec((rows, A_WIDTH), lambda i: (rb0 + i, 0)),
                  pl.BlockSpec((rows, A_WIDTH), lambda i: (rb0 + i, 1)),
                  pl.BlockSpec((A_GROUPS, chunk_len, chunk_len), lambda i: (0, 0, 0)),
                  pl.BlockSpec((chunk_len, A_GROUPS), lambda i: (0, 0)),
                  pl.BlockSpec((1, A_WIDTH), lambda i: (0, 0)),
                  pl.BlockSpec(memory_space=pl.ANY)],
        out_specs=pl.BlockSpec((rows, A_WIDTH), lambda i: (rb0 + i, 0)),
        input_output_aliases={5: 0},
        compiler_params=_cparams(1),
        name="sgu_%d" % chunk_len,
    )(uva, uva, w_s, b_s_t, g_a, mix)


_LOG2E = 1.4426950408889634
_SCALE2 = B_HEAD_DIM ** -0.5 * _LOG2E
_NT = (((1,), (1,)), ((), ()))


def _attn_prompt_kernel(q_ref, *rest):
    k_refs = rest[:ATT_KB]
    v_refs = rest[ATT_KB:2 * ATT_KB]
    bias_ref, gb_ref, _, o_ref, acc_ref = rest[2 * ATT_KB:]
    i = pl.program_id(0)
    for h in range(B_HEADS):
        hs = slice(h * B_HEAD_DIM, (h + 1) * B_HEAD_DIM)
        q = q_ref[:, hs]
        s = []
        for b in range(ATT_KB):
            sb = lax.dot_general(q, k_refs[b][:, hs], _NT,
                                 preferred_element_type=jnp.float32)
            sb = sb * _SCALE2 + bias_ref[h, :, b * ATT_Q:(b + 1) * ATT_Q]
            if b < ATT_KB - 1:
                sb = jnp.where(i + b >= ATT_KB - 1, sb, NEG)
            s.append(sb)
        m = functools.reduce(jnp.maximum,
                             [jnp.max(sb, axis=-1, keepdims=True) for sb in s])
        p = [jnp.exp2(sb - m) for sb in s]
        l = functools.reduce(lambda a, b: a + b,
                             [jnp.sum(pb, axis=-1, keepdims=True) for pb in p])
        o = functools.reduce(lambda a, b: a + b, [
            jnp.dot(p[b].astype(jnp.bfloat16), v_refs[b][:, hs],
                    preferred_element_type=jnp.float32) for b in range(ATT_KB)])
        acc_ref[:, hs] = o / l
    o = acc_ref[...]
    o_ref[...] = (o * _rms_scale(o) * gb_ref[...]).astype(o_ref.dtype)


def _attn_prompt(qkv16, bias, g_b, mix):
    n_blocks = T_PROMPT // ATT_Q

    def kv_spec(b, col):
        return pl.BlockSpec(
            (ATT_Q, B_WIDTH),
            lambda i: (jnp.maximum(i + b - (ATT_KB - 1), 0), col))

    in_specs = ([pl.BlockSpec((ATT_Q, B_WIDTH), lambda i: (i, 0))]
                + [kv_spec(b, 1) for b in range(ATT_KB)]
                + [kv_spec(b, 2) for b in range(ATT_KB)]
                + [pl.BlockSpec((B_HEADS, ATT_Q, ATT_KB * ATT_Q), lambda i: (0, 0, 0)),
                   pl.BlockSpec((1, B_WIDTH), lambda i: (0, 0)),
                   pl.BlockSpec(memory_space=pl.ANY)])
    n_in = len(in_specs)
    return pl.pallas_call(
        _attn_prompt_kernel,
        out_shape=jax.ShapeDtypeStruct((T_ALL, D_MODEL), jnp.bfloat16),
        grid=(n_blocks,),
        in_specs=in_specs,
        out_specs=pl.BlockSpec((ATT_Q, B_WIDTH), lambda i: (i, 1)),
        scratch_shapes=[pltpu.VMEM((ATT_Q, B_WIDTH), jnp.float32)],
        input_output_aliases={n_in - 1: 0},
        compiler_params=_cparams(1),
        name="attn_prompt",
    )(qkv16, *([qkv16] * (2 * ATT_KB)), bias, g_b, mix)


def _attn_sample_kernel(q_ref, kn_ref, vn_ref, kc_ref, vc_ref, bias_ref, gb_ref, _,
                        o_ref, acc_ref):
    n_cache = kc_ref.shape[0]
    for h in range(B_HEADS):
        hs = slice(h * B_HEAD_DIM, (h + 1) * B_HEAD_DIM)
        q = q_ref[:, hs]
        kc = kc_ref[:, hs].astype(jnp.bfloat16)
        vc = vc_ref[:, hs].astype(jnp.bfloat16)
        sc = lax.dot_general(q, kc, _NT, preferred_element_type=jnp.float32)
        sc = sc * _SCALE2 + bias_ref[h, :, :n_cache]
        sn = lax.dot_general(q, kn_ref[:, hs], _NT, preferred_element_type=jnp.float32)
        sn = sn * _SCALE2 + bias_ref[h, :, n_cache:]
        m = jnp.maximum(jnp.max(sc, axis=-1, keepdims=True),
                        jnp.max(sn, axis=-1, keepdims=True))
        pc = jnp.exp2(sc - m)
        pn = jnp.exp2(sn - m)
        l = jnp.sum(pc, axis=-1, keepdims=True) + jnp.sum(pn, axis=-1, keepdims=True)
        o = (jnp.dot(pc.astype(jnp.bfloat16), vc, preferred_element_type=jnp.float32)
             + jnp.dot(pn.astype(jnp.bfloat16), vn_ref[:, hs],
                       preferred_element_type=jnp.float32))
        acc_ref[:, hs] = o / l
    o = acc_ref[...]
    o_ref[...] = (o * _rms_scale(o) * gb_ref[...]).astype(o_ref.dtype)


def _attn_sample(qkv16, cache_k, cache_v, bias, g_b, mix):
    n_cache = cache_k.shape[1]
    rb0 = T_PROMPT // DEC_SEQ
    new_spec = lambda col: pl.BlockSpec((DEC_SEQ, B_WIDTH), lambda b: (rb0 + b, col))
    cache_spec = pl.BlockSpec((None, n_cache, B_WIDTH), lambda b: (b, 0, 0))
    return pl.pallas_call(
        _attn_sample_kernel,
        out_shape=jax.ShapeDtypeStruct((T_ALL, D_MODEL), jnp.bfloat16),
        grid=(DEC_BATCH,),
        in_specs=[new_spec(0), new_spec(1), new_spec(2), cache_spec, cache_spec,
                  pl.BlockSpec((B_HEADS, DEC_SEQ, n_cache + DEC_SEQ), lambda b: (0, 0, 0)),
                  pl.BlockSpec((1, B_WIDTH), lambda b: (0, 0)),
                  pl.BlockSpec(memory_space=pl.ANY)],
        out_specs=pl.BlockSpec((DEC_SEQ, B_WIDTH), lambda b: (rb0 + b, 1)),
        scratch_shapes=[pltpu.VMEM((DEC_SEQ, B_WIDTH), jnp.float32)],
        input_output_aliases={7: 0},
        compiler_params=_cparams(1),
        name="attn_sample",
    )(qkv16, qkv16, qkv16, cache_k, cache_v, bias, g_b, mix)


def _rel_bias_log2(table, n_q, n_k, offset):
    n = n_q + n_k - 1
    d = np.arange(offset - (n_k - 1), offset + n_q)
    v = table[:, np.clip(d, -(CHUNK - 1), REL_MAX) + (CHUNK - 1)].astype(jnp.float32) * _LOG2E
    hankel = jnp.tile(v, (1, n_q + 1))[:, :n_q * (n + 1)].reshape(-1, n_q, n + 1)
    return hankel[:, :, :n_k][:, :, ::-1]


def _prompt_bias(table):
    n_k = ATT_KB * ATT_Q
    bias = _rel_bias_log2(table, ATT_Q, n_k, B_PAST_REACH)
    qc = np.arange(ATT_Q)[:, None] // CHUNK
    kc = np.arange(n_k)[None, :] // CHUNK
    band = (kc >= qc) & (kc <= qc + B_PAST_REACH // CHUNK)
    return jnp.where(band[None], bias, NEG)


def _split_bf16(x):
    hi = x.astype(jnp.bfloat16)
    lo = (x - hi.astype(jnp.float32)).astype(jnp.bfloat16)
    return hi, lo


def _bits(x):
    return lax.bitcast_convert_type(x, jnp.uint32)


def _router_kernel(xp_ref, xs_ref, hp_ref, g_ref, wr_ref, br_ref,
                   h_ref, xt_ref, eid_ref, gate_ref):
    i = pl.program_id(0)

    @pl.when(i < N_PROMPT_TILES)
    def _():
        h_ref[...] = xp_ref[...] + hp_ref[...]

    @pl.when(i >= N_PROMPT_TILES)
    def _():
        h_ref[...] = xs_ref[...] + hp_ref[...]

    h = h_ref[...]
    xt = h * _rms_scale(h) * g_ref[...]
    a = xt[:, :HALF].astype(jnp.bfloat16).astype(jnp.float32)
    b = xt[:, HALF:].astype(jnp.bfloat16).astype(jnp.float32)
    word = _bits(a) | (_bits(b) >> 16)
    for k in range(HALF // LANES):
        xt_ref[:, k, :] = word[:, k * LANES:(k + 1) * LANES]

    xh, xl = _split_bf16(xt)
    wh, wl = _split_bf16(wr_ref[...])
    dot = functools.partial(jnp.dot, preferred_element_type=jnp.float32)
    logits = dot(xh, wh) + (dot(xl, wh) + dot(xh, wl)) + br_ref[...]

    lane = lax.broadcasted_iota(jnp.int32, logits.shape, 1)
    big = jnp.int32(ROUTER_COLS)

    def masked_max(x, mask):
        return jnp.max(jnp.where(mask, x, NEG), axis=-1, keepdims=True)

    def first_lane(mask):
        return jnp.min(jnp.where(mask, lane, big), axis=-1, keepdims=True)

    gmask = lane < N_GROUPS
    gmax = masked_max(logits, gmask)
    gsum = jnp.sum(jnp.where(gmask, jnp.exp(logits - gmax), 0.0), axis=-1, keepdims=True)
    g_sel = first_lane(gmask & (logits == gmax))
    pg_top = 1.0 / gsum

    e_lo = N_GROUPS + g_sel * EXPERTS_PER_GROUP
    emask = (lane >= e_lo) & (lane < e_lo + EXPERTS_PER_GROUP)
    emax = masked_max(logits, emask)
    ex = jnp.where(emask, jnp.exp(logits - emax), 0.0)
    pe = ex / jnp.sum(ex, axis=-1, keepdims=True)
    p1 = jnp.max(pe, axis=-1, keepdims=True)
    l1 = first_lane(emask & (pe == p1))
    rest = emask & (lane != l1)
    p2 = jnp.max(jnp.where(rest, pe, -1.0), axis=-1, keepdims=True)
    l2 = first_lane(rest & (pe == p2))
    denom = p1 + p2
    g1 = pg_top * p1 / denom
    g2 = pg_top * p2 / denom
    eid_ref[...] = jnp.where(lane == 0, l1 - N_GROUPS,
                             jnp.where(lane == 1, l2 - N_GROUPS, 0))
    gate_ref[...] = jnp.where(lane == 0, g1, jnp.where(lane == 1, g2, 0.0))


def _router(xp, xs, hproj, g_ffn, w_router, b_router):
    tile = lambda cols: pl.BlockSpec((ROW_TILE, cols), lambda i: (i, 0))
    return pl.pallas_call(
        _router_kernel,
        out_shape=(jax.ShapeDtypeStruct((T_ALL, D_MODEL), jnp.float32),
                   jax.ShapeDtypeStruct((T_ALL, HALF // LANES, LANES), jnp.uint32),
                   jax.ShapeDtypeStruct((T_ALL, ROUTER_COLS), jnp.int32),
                   jax.ShapeDtypeStruct((T_ALL, ROUTER_COLS), jnp.float32)),
        grid=(N_ROW_TILES,),
        in_specs=[pl.BlockSpec((ROW_TILE, D_MODEL), _prompt_tile),
                  pl.BlockSpec((ROW_TILE, D_MODEL), _sample_tile),
                  tile(D_MODEL),
                  pl.BlockSpec((1, D_MODEL), lambda i: (0, 0)),
                  pl.BlockSpec((D_MODEL, ROUTER_COLS), lambda i: (0, 0)),
                  pl.BlockSpec((1, ROUTER_COLS), lambda i: (0, 0))],
        out_specs=(tile(D_MODEL),
                   pl.BlockSpec((ROW_TILE, HALF // LANES, LANES), lambda i: (i, 0, 0)),
                   tile(ROUTER_COLS), tile(ROUTER_COLS)),
        compiler_params=_cparams(1),
        name="router",
    )(xp, xs, hproj, g_ffn, w_router, b_router)


def _gather_rows_start(src_hbm, idx_ref, n_rows, buf, slot, sem):
    def body(r, carry):
        pltpu.make_async_copy(src_hbm.at[idx_ref[0, 0, r]], buf.at[slot, r],
                              sem.at[slot]).start()
        return carry
    lax.fori_loop(0, n_rows, body, 0)


def _gather_rows_wait(src_hbm, n_rows, buf, slot, sem):
    def body(r, carry):
        pltpu.make_async_copy(src_hbm.at[0], buf.at[slot, r], sem.at[slot]).wait()
        return carry
    lax.fori_loop(0, n_rows, body, 0)


def _rows_2d(buf, slot, rows):
    return jnp.concatenate([buf[slot, rows, k, :] for k in range(buf.shape[2])], axis=-1)


def _unpack_rows(x):
    hi = lax.bitcast_convert_type(x & jnp.uint32(0xFFFF0000), jnp.float32)
    lo = lax.bitcast_convert_type(x << 16, jnp.float32)
    return jnp.concatenate([hi.astype(jnp.bfloat16), lo.astype(jnp.bfloat16)], axis=-1)


def _expert_kernel(ce_ref, cr_ref, nv_ref, tok_ref, tokn_ref, xt_hbm, wg_ref, wu_ref, wd_ref,
                   o_ref, xbuf, sem, *, sub, slab):
    c = pl.program_id(0)
    j = pl.program_id(1)
    nv = nv_ref[0]
    ch = xbuf.shape[1]
    d_model = wg_ref.shape[0]
    tf = wg_ref.shape[1]
    n_sub_max = ch // sub

    def padded_rows(cc):
        return (cr_ref[cc] + (sub - 1)) & (-sub)

    @pl.when((j == 0) & (c == 0))
    def _():
        _gather_rows_start(xt_hbm, tok_ref, padded_rows(0), xbuf, 0, sem)

    @pl.when((j == 0) & (c + 1 < nv))
    def _():
        _gather_rows_start(xt_hbm, tokn_ref, padded_rows(c + 1), xbuf, (c + 1) % 2, sem)

    @pl.when((j == 0) & (c < nv))
    def _():
        _gather_rows_wait(xt_hbm, padded_rows(c), xbuf, c % 2, sem)
        o_ref[...] = jnp.zeros(o_ref.shape, o_ref.dtype)

    def dot16(a, w):
        return jnp.dot(a, w.astype(jnp.bfloat16), preferred_element_type=jnp.float32)

    @pl.when(c < nv)
    def _():
        rows = cr_ref[c]
        slot = c % 2
        for n_sub in range(1, n_sub_max + 1):
            lo, hi = (n_sub - 1) * sub, n_sub * sub
            cond = (rows > lo) & (rows <= hi) if n_sub < n_sub_max else rows > lo

            @pl.when(cond)
            def _():
                m = n_sub * sub
                x = _unpack_rows(_rows_2d(xbuf, slot, slice(0, m)))
                hg = jnp.zeros((m, tf), jnp.float32)
                hu = jnp.zeros((m, tf), jnp.float32)
                for s in range(d_model // slab):
                    ks = slice(s * slab, (s + 1) * slab)
                    hg = hg + dot16(x[:, ks], wg_ref[ks, :])
                    hu = hu + dot16(x[:, ks], wu_ref[ks, :])
                hh = (jax.nn.silu(hg) * hu).astype(jnp.bfloat16)
                for s in range(d_model // slab):
                    y = dot16(hh, wd_ref[:, s * slab:(s + 1) * slab])
                    for kk in range(slab // LANES):
                        k = s * (slab // LANES) + kk
                        o_ref[:m, k, :] += y[:, kk * LANES:(kk + 1) * LANES]


def _experts(chunk_expert, chunk_rows, n_chunks, row_tok, xt3, w_gate, w_up, w_down, *,
             ch=MOE_CH, sub=MOE_SUB, tf=MOE_TF, slab=MOE_SLAB):
    nc = row_tok.shape[0]
    kh = xt3.shape[1]
    d_model, d_expert = w_gate.shape[1], w_gate.shape[2]
    nf = d_expert // tf

    def chunk_of(c, nv):
        return jnp.minimum(c, nv[0] - 1)

    def ftile_of(c, j, nv):
        return jnp.where(c < nv[0], j, nf - 1)

    grid_spec = pltpu.PrefetchScalarGridSpec(
        num_scalar_prefetch=3,
        grid=(nc, nf),
        in_specs=[
            pl.BlockSpec((1, 1, ch), lambda c, j, ce, cr, nv: (chunk_of(c, nv), 0, 0),
                         memory_space=pltpu.SMEM),
            pl.BlockSpec((1, 1, ch), lambda c, j, ce, cr, nv: (jnp.minimum(c + 1, nc - 1), 0, 0),
                         memory_space=pltpu.SMEM),
            pl.BlockSpec(memory_space=pl.ANY),
            pl.BlockSpec((None, d_model, tf),
                         lambda c, j, ce, cr, nv: (ce[chunk_of(c, nv)], 0, ftile_of(c, j, nv))),
            pl.BlockSpec((None, d_model, tf),
                         lambda c, j, ce, cr, nv: (ce[chunk_of(c, nv)], 0, ftile_of(c, j, nv))),
            pl.BlockSpec((None, tf, d_model),
                         lambda c, j, ce, cr, nv: (ce[chunk_of(c, nv)], ftile_of(c, j, nv), 0)),
        ],
        out_specs=pl.BlockSpec((ch, d_model // LANES, LANES),
                               lambda c, j, ce, cr, nv: (chunk_of(c, nv), 0, 0)),
        scratch_shapes=[pltpu.VMEM((2, ch, kh, LANES), jnp.uint32),
                        pltpu.SemaphoreType.DMA((2,))],
    )
    idx = row_tok.reshape(nc, 1, ch)
    return pl.pallas_call(
        functools.partial(_expert_kernel, sub=sub, slab=slab),
        out_shape=jax.ShapeDtypeStruct((nc * ch, d_model // LANES, LANES), jnp.float32),
        grid_spec=grid_spec,
        compiler_params=_cparams(2),
        name="experts",
    )(chunk_expert, chunk_rows, n_chunks, idx, idx, xt3, w_gate, w_up, w_down)


def _moe_plan(eid, *, ch=MOE_CH, nc=MOE_NC, n_experts=N_EXPERTS):
    flat_e = eid.reshape(-1)
    n = flat_e.shape[0]
    order = jnp.argsort(flat_e, stable=True).astype(jnp.int32)
    rank = jnp.argsort(order).astype(jnp.int32)
    counts = jnp.sum(flat_e[:, None] == jnp.arange(n_experts, dtype=jnp.int32)[None, :],
                     axis=0, dtype=jnp.int32)
    start = jnp.cumsum(counts) - counts
    nchunk = (counts + ch - 1) // ch
    cend = jnp.cumsum(nchunk)
    cstart = cend - nchunk
    pos = cstart[flat_e] * ch + rank - start[flat_e]
    n_chunks = cend[-1:]
    cidx = jnp.arange(nc, dtype=jnp.int32)
    chunk_expert = jnp.clip(jnp.searchsorted(cend, cidx, side="right"),
                            0, n_experts - 1).astype(jnp.int32)
    chunk_rows = jnp.clip(counts[chunk_expert] - (cidx - cstart[chunk_expert]) * ch, 0, ch)
    chunk_rows = jnp.where(cidx < n_chunks[0], chunk_rows, 0).astype(jnp.int32)
    r = jnp.arange(ch, dtype=jnp.int32)[None, :]
    src = start[chunk_expert][:, None] + (cidx - cstart[chunk_expert])[:, None] * ch + r
    row_tok = jnp.where(r < chunk_rows[:, None], order[jnp.clip(src, 0, n - 1)] // 2, 0)
    return row_tok, pos, chunk_expert, chunk_rows, n_chunks.astype(jnp.int32)


def _combine_kernel(pos_ref, posn_ref, h_ref, gate_ref, g_ref, y_hbm, op_ref, os_ref,
                    ybuf, sem, *, n_prompt_tiles):
    i = pl.program_id(0)
    n = pl.num_programs(0)
    n_rows = ybuf.shape[1]
    tile = n_rows // 2

    @pl.when(i == 0)
    def _():
        _gather_rows_start(y_hbm, pos_ref, n_rows, ybuf, 0, sem)

    @pl.when(i + 1 < n)
    def _():
        _gather_rows_start(y_hbm, posn_ref, n_rows, ybuf, (i + 1) % 2, sem)

    _gather_rows_wait(y_hbm, n_rows, ybuf, i % 2, sem)
    slot = i % 2
    x = (h_ref[...] + gate_ref[:, 0:1] * _rows_2d(ybuf, slot, slice(0, tile))
         + gate_ref[:, 1:2] * _rows_2d(ybuf, slot, slice(tile, n_rows)))
    y = x * _rms_scale(x) * g_ref[...]

    @pl.when(i < n_prompt_tiles)
    def _():
        op_ref[...] = y

    @pl.when(i >= n_prompt_tiles)
    def _():
        os_ref[...] = y


def _combine(h, y3, pos, gates, g_final, *, t_prompt=T_PROMPT, tile=COMBINE_TILE):
    t_all, d_model = h.shape
    n_tiles = t_all // tile
    n_prompt_tiles = t_prompt // tile
    idx = pos.reshape(n_tiles, tile, 2).transpose(0, 2, 1).reshape(n_tiles, 1, 2 * tile)
    smem = lambda f: pl.BlockSpec((1, 1, 2 * tile), f, memory_space=pltpu.SMEM)
    return pl.pallas_call(
        functools.partial(_combine_kernel, n_prompt_tiles=n_prompt_tiles),
        out_shape=(jax.ShapeDtypeStruct((t_prompt, d_model), jnp.float32),
                   jax.ShapeDtypeStruct((t_all - t_prompt, d_model), jnp.float32)),
        grid=(n_tiles,),
        in_specs=[smem(lambda i: (i, 0, 0)),
                  smem(lambda i: (jnp.minimum(i + 1, n_tiles - 1), 0, 0)),
                  pl.BlockSpec((tile, d_model), lambda i: (i, 0)),
                  pl.BlockSpec((tile, gates.shape[1]), lambda i: (i, 0)),
                  pl.BlockSpec((1, d_model), lambda i: (0, 0)),
                  pl.BlockSpec(memory_space=pl.ANY)],
        out_specs=(pl.BlockSpec((tile, d_model),
                                lambda i: (jnp.minimum(i, n_prompt_tiles - 1), 0)),
                   pl.BlockSpec((tile, d_model),
                                lambda i: (jnp.maximum(i - n_prompt_tiles, 0), 0))),
        scratch_shapes=[pltpu.VMEM((2, 2 * tile, d_model // LANES, LANES), jnp.float32),
                        pltpu.SemaphoreType.DMA((2,))],
        compiler_params=_cparams(1),
        name="combine",
    )(idx, idx, h, gates, g_final, y3)


def kernel(x_prompt, x_sample, cache_b_k, cache_b_v, g_mix, w_in, sgu_v_gain, sgu_w, sgu_b,
           rel_bias, g_out_a, g_out_b, w_out, g_ffn, w_router_group, b_router_group,
           w_router_expert, b_router_expert, w_gate, w_up, w_down, g_final):
    assert w_in.shape[0] == 1, "one layer"
    xp = x_prompt.reshape(T_PROMPT, D_MODEL)
    xs = x_sample.reshape(T_SAMPLE, D_MODEL)
    row = lambda v: v.reshape(1, -1)

    xn = _prenorm(xp, xs, row(g_mix[0]))
    w_in16 = w_in[0].astype(jnp.bfloat16)
    uva = _proj_uva(xn, w_in16, row(sgu_v_gain[0]))
    qkv16, qkv32 = _proj_qkv(xn, w_in16)

    w_s, b_s = sgu_w[0], sgu_b[0]
    mix = _sgu(uva, w_s, b_s.T, row(g_out_a[0]), xn,
               chunk_len=A_CHUNK, n_chunks=SGU_CHUNKS_PER_STEP, row0=0,
               n_steps=T_PROMPT // (A_CHUNK * SGU_CHUNKS_PER_STEP))
    mix = _sgu(uva, w_s[:, :DEC_SEQ, :DEC_SEQ], b_s[:, :DEC_SEQ].T, row(g_out_a[0]), mix,
               chunk_len=DEC_SEQ, n_chunks=DEC_BATCH, row0=T_PROMPT, n_steps=1)

    table = rel_bias[0]
    n_cache = cache_b_k.shape[2]
    mix = _attn_prompt(qkv16, _prompt_bias(table), row(g_out_b[0]), mix)
    bias_s = _rel_bias_log2(table, DEC_SEQ, n_cache + DEC_SEQ, n_cache)
    mix = _attn_sample(qkv16, cache_b_k[0].reshape(DEC_BATCH, n_cache, B_WIDTH),
                       cache_b_v[0].reshape(DEC_BATCH, n_cache, B_WIDTH),
                       bias_s, row(g_out_b[0]), mix)

    hproj = _proj_out(mix, w_out[0].astype(jnp.bfloat16))

    pad = ROUTER_COLS - N_GROUPS - N_EXPERTS
    w_router = jnp.pad(jnp.concatenate([w_router_group[0], w_router_expert[0]], axis=1),
                       ((0, 0), (0, pad)))
    b_router = jnp.pad(jnp.concatenate([b_router_group[0], b_router_expert[0]]), (0, pad))
    h, xt3, eid, gates = _router(xp, xs, hproj, row(g_ffn[0]), w_router, row(b_router))

    row_tok, pos, chunk_expert, chunk_rows, n_chunks = _moe_plan(eid[:, :2])
    y3 = _experts(chunk_expert, chunk_rows, n_chunks, row_tok, xt3,
                  w_gate[0], w_up[0], w_down[0])
    y_prompt, y_sample = _combine(h, y3, pos, gates, row(g_final))

    keep = min(B_PAST_REACH, SEQ)
    kcols = slice(B_WIDTH, 2 * B_WIDTH)
    vcols = slice(2 * B_WIDTH, 3 * B_WIDTH)
    tail = slice(T_PROMPT - keep, T_PROMPT)
    new = slice(T_PROMPT, T_ALL)
    heads = lambda t, lead: t.reshape(*lead, B_HEADS, B_HEAD_DIM)
    return (y_prompt.reshape(1, SEQ, D_MODEL),
            y_sample.reshape(DEC_BATCH, DEC_SEQ, D_MODEL),
            heads(qkv32[tail, kcols], (1, 1, keep)),
            heads(qkv32[tail, vcols], (1, 1, keep)),
            heads(qkv32[new, kcols], (1, DEC_BATCH, DEC_SEQ)),
            heads(qkv32[new, vcols], (1, DEC_BATCH, DEC_SEQ)),
            uva[new, A_WIDTH:].reshape(1, DEC_BATCH, DEC_SEQ, A_WIDTH))
```

```python
import functools

import numpy as np
import jax
import jax.numpy as jnp
from jax import lax
from jax.experimental import pallas as pl
from jax.experimental.pallas import tpu as pltpu

D_MODEL = 4096
SEQ = 8192
DEC_BATCH = 8
DEC_SEQ = 32
T_PROMPT = SEQ
T_SAMPLE = DEC_BATCH * DEC_SEQ
T_ALL = T_PROMPT + T_SAMPLE

CHUNK = 64
A_WIDTH = D_MODEL // 2
A_GROUPS = 4
A_GROUP_DIM = A_WIDTH // A_GROUPS
A_CHUNK = 128
B_WIDTH = D_MODEL - A_WIDTH
B_HEAD_DIM = 128
B_HEADS = B_WIDTH // B_HEAD_DIM
B_PAST_REACH = 8 * CHUNK
REL_MAX = 256
N_GROUPS = 8
EXPERTS_PER_GROUP = 8
N_EXPERTS = N_GROUPS * EXPERTS_PER_GROUP
D_EXPERT = D_MODEL // 4
EPS = 1e-6
NEG = -1e30

LANES = 128
VMEM_LIMIT_BYTES = 56 * 1024 * 1024
ROW_TILE = 256
COMBINE_TILE = 128
MM_TM = 768
MM_TN = 1024
ATT_Q = 256
ATT_KB = B_PAST_REACH // ATT_Q + 1
SGU_CHUNKS_PER_STEP = 4
MOE_CH = 288
MOE_SUB = 96
MOE_KSLAB = 1024
MOE_HSLAB = 256
MOE_SLAB = 1024
MOE_NC = (2 * T_ALL) // MOE_CH + N_EXPERTS
ROUTER_COLS = 128
DMA_LOOP_UNROLL = 8

assert T_ALL % MM_TM == 0 and T_PROMPT % ROW_TILE == 0 and T_SAMPLE == ROW_TILE


def _cparams(n_axes):
    return pltpu.CompilerParams(
        dimension_semantics=("arbitrary",) * n_axes,
        vmem_limit_bytes=VMEM_LIMIT_BYTES)


def _rms_scale(x):
    return lax.rsqrt(jnp.mean(x * x, axis=-1, keepdims=True) + EPS)


N_PROMPT_TILES = T_PROMPT // ROW_TILE
N_ROW_TILES = T_ALL // ROW_TILE


def _prenorm_kernel(xp_ref, xs_ref, g_ref, o_ref):
    i = pl.program_id(0)

    def norm(x):
        return (x * _rms_scale(x) * g_ref[...]).astype(o_ref.dtype)

    @pl.when(i < N_PROMPT_TILES)
    def _():
        o_ref[...] = norm(xp_ref[...])

    @pl.when(i >= N_PROMPT_TILES)
    def _():
        o_ref[...] = norm(xs_ref[...])


def _prompt_tile(i):
    return (jnp.minimum(i, N_PROMPT_TILES - 1), 0)


def _sample_tile(i):
    return (jnp.maximum(i - N_PROMPT_TILES, 0), 0)


def _prenorm(xp, xs, g):
    return pl.pallas_call(
        _prenorm_kernel,
        out_shape=jax.ShapeDtypeStruct((T_ALL, D_MODEL), jnp.bfloat16),
        grid=(N_ROW_TILES,),
        in_specs=[pl.BlockSpec((ROW_TILE, D_MODEL), _prompt_tile),
                  pl.BlockSpec((ROW_TILE, D_MODEL), _sample_tile),
                  pl.BlockSpec((1, D_MODEL), lambda i: (0, 0))],
        out_specs=pl.BlockSpec((ROW_TILE, D_MODEL), lambda i: (i, 0)),
        compiler_params=_cparams(1),
        name="prenorm",
    )(xp, xs, g)


def _uva_kernel(x_ref, w_ref, vg_ref, o_ref):
    j = pl.program_id(0)
    acc = jnp.dot(x_ref[...], w_ref[...], preferred_element_type=jnp.float32)
    g = jax.nn.gelu(acc)

    @pl.when(j < A_WIDTH // MM_TN)
    def _():
        o_ref[...] = g

    @pl.when(j >= A_WIDTH // MM_TN)
    def _():
        for k in range(MM_TN // A_GROUP_DIM):
            cols = slice(k * A_GROUP_DIM, (k + 1) * A_GROUP_DIM)
            gg = g[:, cols]
            o_ref[:, cols] = gg * _rms_scale(gg) * vg_ref[:, cols]


def _qkv_kernel(x_ref, w_ref, o16_ref, o32_ref):
    acc = jnp.dot(x_ref[...], w_ref[...], preferred_element_type=jnp.float32)
    o16_ref[...] = acc.astype(o16_ref.dtype)
    o32_ref[...] = acc


def _plain_mm_kernel(x_ref, w_ref, o_ref):
    o_ref[...] = jnp.dot(x_ref[...], w_ref[...], preferred_element_type=jnp.float32)


def _proj_uva(xn, w_in16, v_gain):
    n_va_tiles = A_WIDTH // MM_TN
    return pl.pallas_call(
        _uva_kernel,
        out_shape=jax.ShapeDtypeStruct((T_ALL, 2 * A_WIDTH), jnp.float32),
        grid=(2 * A_WIDTH // MM_TN, T_ALL // MM_TM),
        in_specs=[pl.BlockSpec((MM_TM, D_MODEL), lambda j, i: (i, 0)),
                  pl.BlockSpec((D_MODEL, MM_TN), lambda j, i: (0, j)),
                  pl.BlockSpec((1, MM_TN), lambda j, i: (0, jnp.maximum(j - n_va_tiles, 0)))],
        out_specs=pl.BlockSpec((MM_TM, MM_TN), lambda j, i: (i, j)),
        compiler_params=_cparams(2),
        name="proj_uva",
    )(xn, w_in16, v_gain)


def _proj_qkv(xn, w_in16):
    col0 = 2 * A_WIDTH // MM_TN
    n_cols = 3 * B_WIDTH
    shape = (T_ALL, n_cols)
    return pl.pallas_call(
        _qkv_kernel,
        out_shape=(jax.ShapeDtypeStruct(shape, jnp.bfloat16),
                   jax.ShapeDtypeStruct(shape, jnp.float32)),
        grid=(n_cols // MM_TN, T_ALL // MM_TM),
        in_specs=[pl.BlockSpec((MM_TM, D_MODEL), lambda j, i: (i, 0)),
                  pl.BlockSpec((D_MODEL, MM_TN), lambda j, i: (0, col0 + j))],
        out_specs=(pl.BlockSpec((MM_TM, MM_TN), lambda j, i: (i, j)),
                   pl.BlockSpec((MM_TM, MM_TN), lambda j, i: (i, j))),
        compiler_params=_cparams(2),
        name="proj_qkv",
    )(xn, w_in16)


def _proj_out(mix, w_out16):
    return pl.pallas_call(
        _plain_mm_kernel,
        out_shape=jax.ShapeDtypeStruct((T_ALL, D_MODEL), jnp.float32),
        grid=(D_MODEL // MM_TN, T_ALL // MM_TM),
        in_specs=[pl.BlockSpec((MM_TM, D_MODEL), lambda j, i: (i, 0)),
                  pl.BlockSpec((D_MODEL, MM_TN), lambda j, i: (0, j))],
        out_specs=pl.BlockSpec((MM_TM, MM_TN), lambda j, i: (i, j)),
        compiler_params=_cparams(2),
        name="proj_out",
    )(mix, w_out16)


def _sgu_kernel(u_ref, v_ref, w_ref, b_ref, ga_ref, *rest, chunk_len, n_chunks):
    o_ref = rest[-1]
    L = chunk_len
    row = lax.broadcasted_iota(jnp.int32, (L, L), 0)
    col = lax.broadcasted_iota(jnp.int32, (L, L), 1)
    w16 = [jnp.where(col <= row, w_ref[g], 0.0).astype(jnp.bfloat16)
           for g in range(A_GROUPS)]
    for c in range(n_chunks):
        rows = slice(c * L, (c + 1) * L)
        parts = []
        ss = jnp.zeros((L, 1), jnp.float32)
        for g in range(A_GROUPS):
            cols = slice(g * A_GROUP_DIM, (g + 1) * A_GROUP_DIM)
            f = jnp.dot(w16[g], v_ref[rows, cols].astype(jnp.bfloat16),
                        preferred_element_type=jnp.float32) + b_ref[:, g:g + 1]
            a = u_ref[rows, cols] * f
            parts.append(a)
            ss = ss + jnp.sum(a * a, axis=-1, keepdims=True)
        r = lax.rsqrt(ss * (1.0 / A_WIDTH) + EPS)
        for g in range(A_GROUPS):
            cols = slice(g * A_GROUP_DIM, (g + 1) * A_GROUP_DIM)
            o_ref[rows, cols] = (parts[g] * r * ga_ref[:, cols]).astype(o_ref.dtype)


def _sgu(uva, w_s, b_s_t, g_a, mix, *, chunk_len, n_chunks, row0, n_steps):
    rows = chunk_len * n_chunks
    rb0 = row0 // rows
    kernel = functools.partial(_sgu_kernel, chunk_len=chunk_len, n_chunks=n_chunks)
    return pl.pallas_call(
        kernel,
        out_shape=jax.ShapeDtypeStruct((T_ALL, D_MODEL), jnp.bfloat16),
        grid=(n_steps,),
        in_specs=[pl.BlockSpec((rows, A_WIDTH), lambda i: (rb0 + i, 0)),
                  pl.BlockSpec((rows, A_WIDTH), lambda i: (rb0 + i, 1)),
                  pl.BlockSpec((A_GROUPS, chunk_len, chunk_len), lambda i: (0, 0, 0)),
                  pl.BlockSpec((chunk_len, A_GROUPS), lambda i: (0, 0)),
                  pl.BlockSpec((1, A_WIDTH), lambda i: (0, 0)),
                  pl.BlockSpec(memory_space=pl.ANY)],
        out_specs=pl.BlockSpec((rows, A_WIDTH), lambda i: (rb0 + i, 0)),
        input_output_aliases={5: 0},
        compiler_params=_cparams(1),
        name="sgu_%d" % chunk_len,
    )(uva, uva, w_s, b_s_t, g_a, mix)


_LOG2E = 1.4426950408889634
_SCALE2 = B_HEAD_DIM ** -0.5 * _LOG2E
_NT = (((1,), (1,)), ((), ()))


def _attn_prompt_kernel(q_ref, *rest):
    k_refs = rest[:ATT_KB]
    v_refs = rest[ATT_KB:2 * ATT_KB]
    bias_ref, gb_ref, _, o_ref, acc_ref = rest[2 * ATT_KB:]
    i = pl.program_id(0)
    for h in range(B_HEADS):
        hs = slice(h * B_HEAD_DIM, (h + 1) * B_HEAD_DIM)
        q = q_ref[:, hs]
        s = []
        for b in range(ATT_KB):
            sb = lax.dot_general(q, k_refs[b][:, hs], _NT,
                                 preferred_element_type=jnp.float32)
            sb = sb * _SCALE2 + bias_ref[h, :, b * ATT_Q:(b + 1) * ATT_Q]
            if b < ATT_KB - 1:
                sb = jnp.where(i + b >= ATT_KB - 1, sb, NEG)
            s.append(sb)
        m = functools.reduce(jnp.maximum,
                             [jnp.max(sb, axis=-1, keepdims=True) for sb in s])
        p = [jnp.exp2(sb - m) for sb in s]
        l = functools.reduce(lambda a, b: a + b,
                             [jnp.sum(pb, axis=-1, keepdims=True) for pb in p])
        o = functools.reduce(lambda a, b: a + b, [
            jnp.dot(p[b].astype(jnp.bfloat16), v_refs[b][:, hs],
                    preferred_element_type=jnp.float32) for b in range(ATT_KB)])
        acc_ref[:, hs] = o / l
    o = acc_ref[...]
    o_ref[...] = (o * _rms_scale(o) * gb_ref[...]).astype(o_ref.dtype)


def _attn_prompt(qkv16, bias, g_b, mix):
    n_blocks = T_PROMPT // ATT_Q

    def kv_spec(b, col):
        return pl.BlockSpec(
            (ATT_Q, B_WIDTH),
            lambda i: (jnp.maximum(i + b - (ATT_KB - 1), 0), col))

    in_specs = ([pl.BlockSpec((ATT_Q, B_WIDTH), lambda i: (i, 0))]
                + [kv_spec(b, 1) for b in range(ATT_KB)]
                + [kv_spec(b, 2) for b in range(ATT_KB)]
                + [pl.BlockSpec((B_HEADS, ATT_Q, ATT_KB * ATT_Q), lambda i: (0, 0, 0)),
                   pl.BlockSpec((1, B_WIDTH), lambda i: (0, 0)),
                   pl.BlockSpec(memory_space=pl.ANY)])
    n_in = len(in_specs)
    return pl.pallas_call(
        _attn_prompt_kernel,
        out_shape=jax.ShapeDtypeStruct((T_ALL, D_MODEL), jnp.bfloat16),
        grid=(n_blocks,),
        in_specs=in_specs,
        out_specs=pl.BlockSpec((ATT_Q, B_WIDTH), lambda i: (i, 1)),
        scratch_shapes=[pltpu.VMEM((ATT_Q, B_WIDTH), jnp.float32)],
        input_output_aliases={n_in - 1: 0},
        compiler_params=_cparams(1),
        name="attn_prompt",
    )(qkv16, *([qkv16] * (2 * ATT_KB)), bias, g_b, mix)


def _attn_sample_kernel(q_ref, kn_ref, vn_ref, kc_ref, vc_ref, bias_ref, gb_ref, _,
                        o_ref, acc_ref):
    n_cache = kc_ref.shape[0] // B_HEADS
    for h in range(B_HEADS):
        hs = slice(h * B_HEAD_DIM, (h + 1) * B_HEAD_DIM)
        q = q_ref[:, hs]
        kc = kc_ref[pl.ds(h, n_cache, stride=B_HEADS), :].astype(jnp.bfloat16)
        vc = vc_ref[pl.ds(h, n_cache, stride=B_HEADS), :].astype(jnp.bfloat16)
        sc = lax.dot_general(q, kc, _NT, preferred_element_type=jnp.float32)
        sc = sc * _SCALE2 + bias_ref[h, :, :n_cache]
        sn = lax.dot_general(q, kn_ref[:, hs], _NT, preferred_element_type=jnp.float32)
        sn = sn * _SCALE2 + bias_ref[h, :, n_cache:]
        m = jnp.maximum(jnp.max(sc, axis=-1, keepdims=True),
                        jnp.max(sn, axis=-1, keepdims=True))
        pc = jnp.exp2(sc - m)
        pn = jnp.exp2(sn - m)
        l = jnp.sum(pc, axis=-1, keepdims=True) + jnp.sum(pn, axis=-1, keepdims=True)
        o = (jnp.dot(pc.astype(jnp.bfloat16), vc, preferred_element_type=jnp.float32)
             + jnp.dot(pn.astype(jnp.bfloat16), vn_ref[:, hs],
                       preferred_element_type=jnp.float32))
        acc_ref[:, hs] = o / l
    o = acc_ref[...]
    o_ref[...] = (o * _rms_scale(o) * gb_ref[...]).astype(o_ref.dtype)


def _attn_sample(qkv16, cache_k, cache_v, bias, g_b, mix):
    n_cache = cache_k.shape[1] // B_HEADS
    rb0 = T_PROMPT // DEC_SEQ
    new_spec = lambda col: pl.BlockSpec((DEC_SEQ, B_WIDTH), lambda b: (rb0 + b, col))
    cache_spec = pl.BlockSpec((None, n_cache * B_HEADS, B_HEAD_DIM), lambda b: (b, 0, 0))
    return pl.pallas_call(
        _attn_sample_kernel,
        out_shape=jax.ShapeDtypeStruct((T_ALL, D_MODEL), jnp.bfloat16),
        grid=(DEC_BATCH,),
        in_specs=[new_spec(0), new_spec(1), new_spec(2), cache_spec, cache_spec,
                  pl.BlockSpec((B_HEADS, DEC_SEQ, n_cache + DEC_SEQ), lambda b: (0, 0, 0)),
                  pl.BlockSpec((1, B_WIDTH), lambda b: (0, 0)),
                  pl.BlockSpec(memory_space=pl.ANY)],
        out_specs=pl.BlockSpec((DEC_SEQ, B_WIDTH), lambda b: (rb0 + b, 1)),
        scratch_shapes=[pltpu.VMEM((DEC_SEQ, B_WIDTH), jnp.float32)],
        input_output_aliases={7: 0},
        compiler_params=_cparams(1),
        name="attn_sample",
    )(qkv16, qkv16, qkv16, cache_k, cache_v, bias, g_b, mix)


def _rel_bias_log2(table, n_q, n_k, offset):
    n = n_q + n_k - 1
    t = np.concatenate([np.arange(0, n_k), np.arange(-(n_q - 1), 0)])
    v = table[:, np.clip(offset - t, -(CHUNK - 1), REL_MAX) + (CHUNK - 1)].astype(jnp.float32)
    v = v * _LOG2E
    toeplitz = jnp.tile(v, (1, n_q))[:, :n_q * (n - 1)].reshape(-1, n_q, n - 1)
    return toeplitz[:, :, :n_k]


def _prompt_bias(table):
    n_k = ATT_KB * ATT_Q
    bias = _rel_bias_log2(table, ATT_Q, n_k, B_PAST_REACH)
    qc = np.arange(ATT_Q)[:, None] // CHUNK
    kc = np.arange(n_k)[None, :] // CHUNK
    band = (kc >= qc) & (kc <= qc + B_PAST_REACH // CHUNK)
    return jnp.where(band[None], bias, NEG)


def _split_bf16(x):
    hi = x.astype(jnp.bfloat16)
    lo = (x - hi.astype(jnp.float32)).astype(jnp.bfloat16)
    return hi, lo


def _router_kernel(xp_ref, xs_ref, hp_ref, g_ref, wr_ref, br_ref,
                   h_ref, xt_ref, eid_ref, gate_ref):
    i = pl.program_id(0)

    @pl.when(i < N_PROMPT_TILES)
    def _():
        h_ref[...] = xp_ref[...] + hp_ref[...]

    @pl.when(i >= N_PROMPT_TILES)
    def _():
        h_ref[...] = xs_ref[...] + hp_ref[...]

    h = h_ref[...]
    xt = h * _rms_scale(h) * g_ref[...]
    kd = D_MODEL // LANES
    for k in range(_pitch(kd)):
        piece = (xt[:, k * LANES:(k + 1) * LANES] if k < kd
                 else jnp.zeros((ROW_TILE, LANES), jnp.float32))
        xt_ref[pl.ds(k, ROW_TILE, stride=_pitch(kd)), :] = piece

    xh, xl = _split_bf16(xt)
    wh, wl = _split_bf16(wr_ref[...])
    dot = functools.partial(jnp.dot, preferred_element_type=jnp.float32)
    logits = dot(xh, wh) + (dot(xl, wh) + dot(xh, wl)) + br_ref[...]

    lane = lax.broadcasted_iota(jnp.int32, logits.shape, 1)
    big = jnp.int32(ROUTER_COLS)

    def masked_max(x, mask):
        return jnp.max(jnp.where(mask, x, NEG), axis=-1, keepdims=True)

    def first_lane(mask):
        return jnp.min(jnp.where(mask, lane, big), axis=-1, keepdims=True)

    gmask = lane < N_GROUPS
    gmax = masked_max(logits, gmask)
    gsum = jnp.sum(jnp.where(gmask, jnp.exp(logits - gmax), 0.0), axis=-1, keepdims=True)
    g_sel = first_lane(gmask & (logits == gmax))
    pg_top = 1.0 / gsum

    e_lo = N_GROUPS + g_sel * EXPERTS_PER_GROUP
    emask = (lane >= e_lo) & (lane < e_lo + EXPERTS_PER_GROUP)
    emax = masked_max(logits, emask)
    ex = jnp.where(emask, jnp.exp(logits - emax), 0.0)
    pe = ex / jnp.sum(ex, axis=-1, keepdims=True)
    p1 = jnp.max(pe, axis=-1, keepdims=True)
    l1 = first_lane(emask & (pe == p1))
    rest = emask & (lane != l1)
    p2 = jnp.max(jnp.where(rest, pe, -1.0), axis=-1, keepdims=True)
    l2 = first_lane(rest & (pe == p2))
    denom = p1 + p2
    g1 = pg_top * p1 / denom
    g2 = pg_top * p2 / denom
    eid_ref[...] = jnp.where(lane == 0, l1 - N_GROUPS,
                             jnp.where(lane == 1, l2 - N_GROUPS, 0))
    gate_ref[...] = jnp.where(lane == 0, g1, jnp.where(lane == 1, g2, 0.0))


def _router(xp, xs, hproj, g_ffn, w_router, b_router):
    tile = lambda cols: pl.BlockSpec((ROW_TILE, cols), lambda i: (i, 0))
    return pl.pallas_call(
        _router_kernel,
        out_shape=(jax.ShapeDtypeStruct((T_ALL, D_MODEL), jnp.float32),
                   jax.ShapeDtypeStruct((T_ALL * _pitch(D_MODEL // LANES), LANES), jnp.float32),
                   jax.ShapeDtypeStruct((T_ALL, ROUTER_COLS), jnp.int32),
                   jax.ShapeDtypeStruct((T_ALL, ROUTER_COLS), jnp.float32)),
        grid=(N_ROW_TILES,),
        in_specs=[pl.BlockSpec((ROW_TILE, D_MODEL), _prompt_tile),
                  pl.BlockSpec((ROW_TILE, D_MODEL), _sample_tile),
                  tile(D_MODEL),
                  pl.BlockSpec((1, D_MODEL), lambda i: (0, 0)),
                  pl.BlockSpec((D_MODEL, ROUTER_COLS), lambda i: (0, 0)),
                  pl.BlockSpec((1, ROUTER_COLS), lambda i: (0, 0))],
        out_specs=(tile(D_MODEL),
                   pl.BlockSpec((ROW_TILE * _pitch(D_MODEL // LANES), LANES), lambda i: (i, 0)),
                   tile(ROUTER_COLS), tile(ROUTER_COLS)),
        compiler_params=_cparams(1),
        name="router",
    )(xp, xs, hproj, g_ffn, w_router, b_router)


def _pitch(k):
    return k + 4 if k % 8 == 0 else k


def _row_copy(src_hbm, src_row, buf, slot, dst_row, sem, k):
    p = _pitch(k)
    return pltpu.make_async_copy(src_hbm.at[pl.ds(src_row * p, k)],
                                 buf.at[slot, pl.ds(dst_row * p, k)], sem.at[slot])


def _gather_rows_start(src_hbm, row_of, row0, n_rows, buf, slot, sem, k):
    def body(r, carry):
        _row_copy(src_hbm, row_of(row0 + r), buf, slot, row0 + r, sem, k).start()
        return carry
    lax.fori_loop(0, n_rows, body, 0, unroll=DMA_LOOP_UNROLL)


def _gather_rows_wait(src_hbm, row0, n_rows, buf, slot, sem, k):
    def body(r, carry):
        _row_copy(src_hbm, 0, buf, slot, row0 + r, sem, k).wait()
        return carry
    lax.fori_loop(0, n_rows, body, 0, unroll=DMA_LOOP_UNROLL)


def _rows_2d(buf, slot, row0, m, k):
    p = _pitch(k)
    return jnp.concatenate(
        [buf[slot, pl.ds(row0 * p + kk, m, stride=p), :] for kk in range(k)], axis=-1)


def _expert_kernel(ce_ref, cm_ref, cb_ref, nv_ref, tok_ref, xt_hbm, wg_ref, wu_ref, wd_ref,
                   o_ref, xbuf, hg_ref, hu_ref, hh_ref, sem, *, ch, sub, slab):
    c = pl.program_id(0)
    j = pl.program_id(1)
    nv = nv_ref[0]
    ks, d_expert = wg_ref.shape
    hs, d_model = wd_ref.shape
    n_a = d_model // ks
    kd = d_model // LANES
    pd = _pitch(kd)
    last_tok = tok_ref.shape[0] - 1

    def row_groups(cc, fn):
        for g in range(ch // sub):
            @pl.when(g * sub < cm_ref[cc])
            def _():
                fn(g * sub)

    def start_gather(cc):
        base = cb_ref[cc]
        row_groups(cc, lambda r0: _gather_rows_start(
            xt_hbm, lambda r: tok_ref[jnp.minimum(base + r, last_tok)], r0, sub,
            xbuf, 0, sem, kd))

    @pl.when((j == 0) & (c == 0))
    def _():
        start_gather(0)

    @pl.when((j == n_a) & (c + 1 < nv))
    def _():
        start_gather(c + 1)

    @pl.when((j == 0) & (c < nv))
    def _():
        row_groups(c, lambda r0: _gather_rows_wait(xt_hbm, r0, sub, xbuf, 0, sem, kd))
        o_ref[...] = jnp.zeros(o_ref.shape, o_ref.dtype)
        hg_ref[...] = jnp.zeros(hg_ref.shape, hg_ref.dtype)
        hu_ref[...] = jnp.zeros(hu_ref.shape, hu_ref.dtype)

    def dot16(a, w):
        return jnp.dot(a, w.astype(jnp.bfloat16), preferred_element_type=jnp.float32)

    @pl.when(c < nv)
    def _():
        for m in range(sub, ch + 1, sub):
            @pl.when((cm_ref[c] == m) & (j < n_a))
            def _():
                k0 = j * (ks // LANES)
                x = jnp.concatenate(
                    [xbuf[0, pl.ds(k0 + kk, m, stride=pd), :] for kk in range(ks // LANES)],
                    axis=-1).astype(jnp.bfloat16)
                hg_ref[:m, :] += dot16(x, wg_ref[...])
                hu_ref[:m, :] += dot16(x, wu_ref[...])

            @pl.when((cm_ref[c] == m) & (j == n_a))
            def _():
                for s in range(d_expert // hs):
                    cols = slice(s * hs, (s + 1) * hs)
                    hh_ref[s, :m, :] = (jax.nn.silu(hg_ref[:m, cols])
                                        * hu_ref[:m, cols]).astype(hh_ref.dtype)

            @pl.when((cm_ref[c] == m) & (j >= n_a))
            def _():
                hh = hh_ref[j - n_a, :m, :]
                for s in range(d_model // slab):
                    y = dot16(hh, wd_ref[:, s * slab:(s + 1) * slab])
                    for kk in range(slab // LANES):
                        k = s * (slab // LANES) + kk
                        o_ref[pl.ds(k, m, stride=pd), :] += y[:, kk * LANES:(kk + 1) * LANES]


def _experts(plan, xt_rows, w_gate, w_up, w_down, *,
             ch=MOE_CH, sub=MOE_SUB, kslab=MOE_KSLAB, hslab=MOE_HSLAB, slab=MOE_SLAB):
    chunk_expert, chunk_mrows, chunk_base, n_chunks, sorted_tok = plan
    nc = chunk_expert.shape[0]
    d_model, d_expert = w_gate.shape[1], w_gate.shape[2]
    n_a = d_model // kslab
    n_b = d_expert // hslab
    pd = _pitch(d_model // LANES)

    def chunk_of(c, nv):
        return jnp.minimum(c, nv[0] - 1)

    def step_of(c, j, nv):
        return jnp.where(c < nv[0], j, n_a + n_b - 1)

    def in_slab(c, j, ce, cm, cb, nv, tok):
        return (ce[chunk_of(c, nv)], jnp.minimum(step_of(c, j, nv), n_a - 1), 0)

    def down_slab(c, j, ce, cm, cb, nv, tok):
        return (ce[chunk_of(c, nv)], jnp.maximum(step_of(c, j, nv) - n_a, 0), 0)

    grid_spec = pltpu.PrefetchScalarGridSpec(
        num_scalar_prefetch=5,
        grid=(nc, n_a + n_b),
        in_specs=[
            pl.BlockSpec(memory_space=pl.ANY),
            pl.BlockSpec((None, kslab, d_expert), in_slab),
            pl.BlockSpec((None, kslab, d_expert), in_slab),
            pl.BlockSpec((None, hslab, d_model), down_slab),
        ],
        out_specs=pl.BlockSpec((ch * pd, LANES),
                               lambda c, j, ce, cm, cb, nv, tok: (chunk_of(c, nv), 0)),
        scratch_shapes=[pltpu.VMEM((1, ch * pd, LANES), jnp.float32),
                        pltpu.VMEM((ch, d_expert), jnp.float32),
                        pltpu.VMEM((ch, d_expert), jnp.float32),
                        pltpu.VMEM((n_b, ch, hslab), jnp.bfloat16),
                        pltpu.SemaphoreType.DMA((1,))],
    )
    return pl.pallas_call(
        functools.partial(_expert_kernel, ch=ch, sub=sub, slab=slab),
        out_shape=jax.ShapeDtypeStruct((nc * ch * pd, LANES), jnp.float32),
        grid_spec=grid_spec,
        compiler_params=_cparams(2),
        name="experts",
    )(chunk_expert, chunk_mrows, chunk_base, n_chunks, sorted_tok, xt_rows,
      w_gate, w_up, w_down)


def _moe_plan(eid, *, ch=MOE_CH, sub=MOE_SUB, nc=MOE_NC, n_experts=N_EXPERTS):
    i32 = jnp.int32
    flat_e = eid.reshape(-1)
    order = jnp.argsort(flat_e, stable=True).astype(i32)
    rank = jnp.argsort(order).astype(i32)
    onehot_e = flat_e[:, None] == jnp.arange(n_experts, dtype=i32)[None, :]
    counts = jnp.sum(onehot_e, axis=0, dtype=i32)
    start = jnp.cumsum(counts) - counts
    nchunk = (counts + ch - 1) // ch
    cend = jnp.cumsum(nchunk)
    cstart = cend - nchunk
    pos = rank + jnp.sum(jnp.where(onehot_e, (cstart * ch - start)[None, :], 0), axis=1)
    n_chunks = cend[-1:]
    cidx = jnp.arange(nc, dtype=i32)
    chunk_expert = jnp.minimum(jnp.sum(cend[None, :] <= cidx[:, None], axis=1, dtype=i32),
                               n_experts - 1)
    onehot_c = chunk_expert[:, None] == jnp.arange(n_experts, dtype=i32)[None, :]
    lookup = lambda table: jnp.sum(jnp.where(onehot_c, table[None, :], 0), axis=1)
    offset = (cidx - lookup(cstart)) * ch
    chunk_rows = jnp.clip(lookup(counts) - offset, 0, ch)
    chunk_rows = jnp.where(cidx < n_chunks[0], chunk_rows, 0)
    chunk_mrows = (chunk_rows + sub - 1) // sub * sub
    chunk_base = lookup(start) + offset
    return (chunk_expert, chunk_mrows.astype(i32), chunk_base.astype(i32),
            n_chunks.astype(i32), order // 2), pos


def _combine_kernel(pos_ref, posn_ref, h_ref, gate_ref, g_ref, y_hbm, op_ref, os_ref,
                    ybuf, sem, *, n_prompt_tiles):
    i = pl.program_id(0)
    n = pl.num_programs(0)
    tile = h_ref.shape[0]
    n_rows = 2 * tile
    kd = h_ref.shape[1] // LANES

    @pl.when(i == 0)
    def _():
        _gather_rows_start(y_hbm, lambda r: pos_ref[0, 0, r], 0, n_rows, ybuf, 0, sem, kd)

    @pl.when(i + 1 < n)
    def _():
        _gather_rows_start(y_hbm, lambda r: posn_ref[0, 0, r], 0, n_rows, ybuf, (i + 1) % 2,
                           sem, kd)

    _gather_rows_wait(y_hbm, 0, n_rows, ybuf, i % 2, sem, kd)
    slot = i % 2
    x = (h_ref[...] + gate_ref[:, 0:1] * _rows_2d(ybuf, slot, 0, tile, kd)
         + gate_ref[:, 1:2] * _rows_2d(ybuf, slot, tile, tile, kd))
    y = x * _rms_scale(x) * g_ref[...]

    @pl.when(i < n_prompt_tiles)
    def _():
        op_ref[...] = y

    @pl.when(i >= n_prompt_tiles)
    def _():
        os_ref[...] = y


def _combine(h, y_rows, pos, gates, g_final, *, t_prompt=T_PROMPT, tile=COMBINE_TILE):
    t_all, d_model = h.shape
    n_tiles = t_all // tile
    n_prompt_tiles = t_prompt // tile
    idx = pos.reshape(n_tiles, tile, 2).transpose(0, 2, 1).reshape(n_tiles, 1, 2 * tile)
    smem = lambda f: pl.BlockSpec((1, 1, 2 * tile), f, memory_space=pltpu.SMEM)
    return pl.pallas_call(
        functools.partial(_combine_kernel, n_prompt_tiles=n_prompt_tiles),
        out_shape=(jax.ShapeDtypeStruct((t_prompt, d_model), jnp.float32),
                   jax.ShapeDtypeStruct((t_all - t_prompt, d_model), jnp.float32)),
        grid=(n_tiles,),
        in_specs=[smem(lambda i: (i, 0, 0)),
                  smem(lambda i: (jnp.minimum(i + 1, n_tiles - 1), 0, 0)),
                  pl.BlockSpec((tile, d_model), lambda i: (i, 0)),
                  pl.BlockSpec((tile, gates.shape[1]), lambda i: (i, 0)),
                  pl.BlockSpec((1, d_model), lambda i: (0, 0)),
                  pl.BlockSpec(memory_space=pl.ANY)],
        out_specs=(pl.BlockSpec((tile, d_model),
                                lambda i: (jnp.minimum(i, n_prompt_tiles - 1), 0)),
                   pl.BlockSpec((tile, d_model),
                                lambda i: (jnp.maximum(i - n_prompt_tiles, 0), 0))),
        scratch_shapes=[pltpu.VMEM((2, 2 * tile * _pitch(d_model // LANES), LANES),
                                   jnp.float32),
                        pltpu.SemaphoreType.DMA((2,))],
        compiler_params=_cparams(1),
        name="combine",
    )(idx, idx, h, gates, g_final, y_rows)


def kernel(x_prompt, x_sample, cache_b_k, cache_b_v, g_mix, w_in, sgu_v_gain, sgu_w, sgu_b,
           rel_bias, g_out_a, g_out_b, w_out, g_ffn, w_router_group, b_router_group,
           w_router_expert, b_router_expert, w_gate, w_up, w_down, g_final):
    assert w_in.shape[0] == 1, "one layer"
    xp = x_prompt.reshape(T_PROMPT, D_MODEL)
    xs = x_sample.reshape(T_SAMPLE, D_MODEL)
    row = lambda v: v.reshape(1, -1)

    xn = _prenorm(xp, xs, row(g_mix[0]))
    w_in16 = w_in[0].astype(jnp.bfloat16)
    uva = _proj_uva(xn, w_in16, row(sgu_v_gain[0]))
    qkv16, qkv32 = _proj_qkv(xn, w_in16)

    w_s, b_s = sgu_w[0], sgu_b[0]
    mix = _sgu(uva, w_s, b_s.T, row(g_out_a[0]), xn,
               chunk_len=A_CHUNK, n_chunks=SGU_CHUNKS_PER_STEP, row0=0,
               n_steps=T_PROMPT // (A_CHUNK * SGU_CHUNKS_PER_STEP))
    mix = _sgu(uva, w_s[:, :DEC_SEQ, :DEC_SEQ], b_s[:, :DEC_SEQ].T, row(g_out_a[0]), mix,
               chunk_len=DEC_SEQ, n_chunks=DEC_BATCH, row0=T_PROMPT, n_steps=1)

    table = rel_bias[0]
    n_cache = cache_b_k.shape[2]
    mix = _attn_prompt(qkv16, _prompt_bias(table), row(g_out_b[0]), mix)
    bias_s = _rel_bias_log2(table, DEC_SEQ, n_cache + DEC_SEQ, n_cache)
    mix = _attn_sample(qkv16, cache_b_k[0].reshape(DEC_BATCH, n_cache * B_HEADS, B_HEAD_DIM),
                       cache_b_v[0].reshape(DEC_BATCH, n_cache * B_HEADS, B_HEAD_DIM),
                       bias_s, row(g_out_b[0]), mix)

    hproj = _proj_out(mix, w_out[0].astype(jnp.bfloat16))

    pad = ROUTER_COLS - N_GROUPS - N_EXPERTS
    w_router = jnp.pad(jnp.concatenate([w_router_group[0], w_router_expert[0]], axis=1),
                       ((0, 0), (0, pad)))
    b_router = jnp.pad(jnp.concatenate([b_router_group[0], b_router_expert[0]]), (0, pad))
    h, xt_rows, eid, gates = _router(xp, xs, hproj, row(g_ffn[0]), w_router, row(b_router))

    plan, pos = _moe_plan(eid[:, :2])
    y_rows = _experts(plan, xt_rows, w_gate[0], w_up[0], w_down[0])
    y_prompt, y_sample = _combine(h, y_rows, pos, gates, row(g_final))

    keep = min(B_PAST_REACH, SEQ)
    kcols = slice(B_WIDTH, 2 * B_WIDTH)
    vcols = slice(2 * B_WIDTH, 3 * B_WIDTH)
    tail = slice(T_PROMPT - keep, T_PROMPT)
    new = slice(T_PROMPT, T_ALL)
    heads = lambda t, lead: t.reshape(*lead, B_HEADS, B_HEAD_DIM)
    return (y_prompt.reshape(1, SEQ, D_MODEL),
            y_sample.reshape(DEC_BATCH, DEC_SEQ, D_MODEL),
            heads(qkv32[tail, kcols], (1, 1, keep)),
            heads(qkv32[tail, vcols], (1, 1, keep)),
            heads(qkv32[new, kcols], (1, DEC_BATCH, DEC_SEQ)),
            heads(qkv32[new, vcols], (1, DEC_BATCH, DEC_SEQ)),
            uva[new, A_WIDTH:].reshape(1, DEC_BATCH, DEC_SEQ, A_WIDTH))
```

```python
import functools

import numpy as np
import jax
import jax.numpy as jnp
from jax import lax
from jax.experimental import pallas as pl
from jax.experimental.pallas import tpu as pltpu

D_MODEL = 4096
SEQ = 8192
DEC_BATCH = 8
DEC_SEQ = 32
T_PROMPT = SEQ
T_SAMPLE = DEC_BATCH * DEC_SEQ
T_ALL = T_PROMPT + T_SAMPLE

CHUNK = 64
A_WIDTH = D_MODEL // 2
A_GROUPS = 4
A_GROUP_DIM = A_WIDTH // A_GROUPS
A_CHUNK = 128
B_WIDTH = D_MODEL - A_WIDTH
B_HEAD_DIM = 128
B_HEADS = B_WIDTH // B_HEAD_DIM
B_PAST_REACH = 8 * CHUNK
REL_MAX = 256
N_GROUPS = 8
EXPERTS_PER_GROUP = 8
N_EXPERTS = N_GROUPS * EXPERTS_PER_GROUP
D_EXPERT = D_MODEL // 4
EPS = 1e-6
NEG = -1e30

LANES = 128
VMEM_LIMIT_BYTES = 56 * 1024 * 1024
ROW_TILE = 256
COMBINE_TILE = 128
MM_TM = 768
MM_TN = 1024
ATT_Q = 256
ATT_KB = B_PAST_REACH // ATT_Q + 1
SGU_CHUNKS_PER_STEP = 4
MOE_CH = 288
MOE_SUB = 96
MOE_TF = 256
MOE_SLAB = 1024
MOE_NC = (2 * T_ALL) // MOE_CH + N_EXPERTS
ROUTER_COLS = 128
DMA_LOOP_UNROLL = 8

assert T_ALL % MM_TM == 0 and T_PROMPT % ROW_TILE == 0 and T_SAMPLE == ROW_TILE


def _cparams(n_axes):
    return pltpu.CompilerParams(
        dimension_semantics=("arbitrary",) * n_axes,
        vmem_limit_bytes=VMEM_LIMIT_BYTES)


def _rms_scale(x):
    return lax.rsqrt(jnp.mean(x * x, axis=-1, keepdims=True) + EPS)


N_PROMPT_TILES = T_PROMPT // ROW_TILE
N_ROW_TILES = T_ALL // ROW_TILE


def _prenorm_kernel(xp_ref, xs_ref, g_ref, o_ref):
    i = pl.program_id(0)

    def norm(x):
        return (x * _rms_scale(x) * g_ref[...]).astype(o_ref.dtype)

    @pl.when(i < N_PROMPT_TILES)
    def _():
        o_ref[...] = norm(xp_ref[...])

    @pl.when(i >= N_PROMPT_TILES)
    def _():
        o_ref[...] = norm(xs_ref[...])


def _prompt_tile(i):
    return (jnp.minimum(i, N_PROMPT_TILES - 1), 0)


def _sample_tile(i):
    return (jnp.maximum(i - N_PROMPT_TILES, 0), 0)


def _prenorm(xp, xs, g):
    return pl.pallas_call(
        _prenorm_kernel,
        out_shape=jax.ShapeDtypeStruct((T_ALL, D_MODEL), jnp.bfloat16),
        grid=(N_ROW_TILES,),
        in_specs=[pl.BlockSpec((ROW_TILE, D_MODEL), _prompt_tile),
                  pl.BlockSpec((ROW_TILE, D_MODEL), _sample_tile),
                  pl.BlockSpec((1, D_MODEL), lambda i: (0, 0))],
        out_specs=pl.BlockSpec((ROW_TILE, D_MODEL), lambda i: (i, 0)),
        compiler_params=_cparams(1),
        name="prenorm",
    )(xp, xs, g)


def _uva_kernel(x_ref, w_ref, vg_ref, o_ref):
    j = pl.program_id(0)
    acc = jnp.dot(x_ref[...], w_ref[...], preferred_element_type=jnp.float32)
    g = jax.nn.gelu(acc)

    @pl.when(j < A_WIDTH // MM_TN)
    def _():
        o_ref[...] = g

    @pl.when(j >= A_WIDTH // MM_TN)
    def _():
        for k in range(MM_TN // A_GROUP_DIM):
            cols = slice(k * A_GROUP_DIM, (k + 1) * A_GROUP_DIM)
            gg = g[:, cols]
            o_ref[:, cols] = gg * _rms_scale(gg) * vg_ref[:, cols]


def _qkv_kernel(x_ref, w_ref, o16_ref, o32_ref):
    acc = jnp.dot(x_ref[...], w_ref[...], preferred_element_type=jnp.float32)
    o16_ref[...] = acc.astype(o16_ref.dtype)
    o32_ref[...] = acc


def _plain_mm_kernel(x_ref, w_ref, o_ref):
    o_ref[...] = jnp.dot(x_ref[...], w_ref[...], preferred_element_type=jnp.float32)


def _proj_uva(xn, w_in16, v_gain):
    n_va_tiles = A_WIDTH // MM_TN
    return pl.pallas_call(
        _uva_kernel,
        out_shape=jax.ShapeDtypeStruct((T_ALL, 2 * A_WIDTH), jnp.float32),
        grid=(2 * A_WIDTH // MM_TN, T_ALL // MM_TM),
        in_specs=[pl.BlockSpec((MM_TM, D_MODEL), lambda j, i: (i, 0)),
                  pl.BlockSpec((D_MODEL, MM_TN), lambda j, i: (0, j)),
                  pl.BlockSpec((1, MM_TN), lambda j, i: (0, jnp.maximum(j - n_va_tiles, 0)))],
        out_specs=pl.BlockSpec((MM_TM, MM_TN), lambda j, i: (i, j)),
        compiler_params=_cparams(2),
        name="proj_uva",
    )(xn, w_in16, v_gain)


def _proj_qkv(xn, w_in16):
    col0 = 2 * A_WIDTH // MM_TN
    n_cols = 3 * B_WIDTH
    shape = (T_ALL, n_cols)
    return pl.pallas_call(
        _qkv_kernel,
        out_shape=(jax.ShapeDtypeStruct(shape, jnp.bfloat16),
                   jax.ShapeDtypeStruct(shape, jnp.float32)),
        grid=(n_cols // MM_TN, T_ALL // MM_TM),
        in_specs=[pl.BlockSpec((MM_TM, D_MODEL), lambda j, i: (i, 0)),
                  pl.BlockSpec((D_MODEL, MM_TN), lambda j, i: (0, col0 + j))],
        out_specs=(pl.BlockSpec((MM_TM, MM_TN), lambda j, i: (i, j)),
                   pl.BlockSpec((MM_TM, MM_TN), lambda j, i: (i, j))),
        compiler_params=_cparams(2),
        name="proj_qkv",
    )(xn, w_in16)


def _proj_out(mix, w_out16):
    return pl.pallas_call(
        _plain_mm_kernel,
        out_shape=jax.ShapeDtypeStruct((T_ALL, D_MODEL), jnp.float32),
        grid=(D_MODEL // MM_TN, T_ALL // MM_TM),
        in_specs=[pl.BlockSpec((MM_TM, D_MODEL), lambda j, i: (i, 0)),
                  pl.BlockSpec((D_MODEL, MM_TN), lambda j, i: (0, j))],
        out_specs=pl.BlockSpec((MM_TM, MM_TN), lambda j, i: (i, j)),
        compiler_params=_cparams(2),
        name="proj_out",
    )(mix, w_out16)


def _sgu_kernel(u_ref, v_ref, w_ref, b_ref, ga_ref, *rest, chunk_len, n_chunks):
    o_ref = rest[-1]
    L = chunk_len
    row = lax.broadcasted_iota(jnp.int32, (L, L), 0)
    col = lax.broadcasted_iota(jnp.int32, (L, L), 1)
    w16 = [jnp.where(col <= row, w_ref[g], 0.0).astype(jnp.bfloat16)
           for g in range(A_GROUPS)]
    for c in range(n_chunks):
        rows = slice(c * L, (c + 1) * L)
        parts = []
        ss = jnp.zeros((L, 1), jnp.float32)
        for g in range(A_GROUPS):
            cols = slice(g * A_GROUP_DIM, (g + 1) * A_GROUP_DIM)
            f = jnp.dot(w16[g], v_ref[rows, cols].astype(jnp.bfloat16),
                        preferred_element_type=jnp.float32) + b_ref[:, g:g + 1]
            a = u_ref[rows, cols] * f
            parts.append(a)
            ss = ss + jnp.sum(a * a, axis=-1, keepdims=True)
        r = lax.rsqrt(ss * (1.0 / A_WIDTH) + EPS)
        for g in range(A_GROUPS):
            cols = slice(g * A_GROUP_DIM, (g + 1) * A_GROUP_DIM)
            o_ref[rows, cols] = (parts[g] * r * ga_ref[:, cols]).astype(o_ref.dtype)


def _sgu(uva, w_s, b_s_t, g_a, mix, *, chunk_len, n_chunks, row0, n_steps):
    rows = chunk_len * n_chunks
    rb0 = row0 // rows
    kernel = functools.partial(_sgu_kernel, chunk_len=chunk_len, n_chunks=n_chunks)
    return pl.pallas_call(
        kernel,
        out_shape=jax.ShapeDtypeStruct((T_ALL, D_MODEL), jnp.bfloat16),
        grid=(n_steps,),
        in_specs=[pl.BlockSpec((rows, A_WIDTH), lambda i: (rb0 + i, 0)),
                  pl.BlockSpec((rows, A_WIDTH), lambda i: (rb0 + i, 1)),
                  pl.BlockSpec((A_GROUPS, chunk_len, chunk_len), lambda i: (0, 0, 0)),
                  pl.BlockSpec((chunk_len, A_GROUPS), lambda i: (0, 0)),
                  pl.BlockSpec((1, A_WIDTH), lambda i: (0, 0)),
                  pl.BlockSpec(memory_space=pl.ANY)],
        out_specs=pl.BlockSpec((rows, A_WIDTH), lambda i: (rb0 + i, 0)),
        input_output_aliases={5: 0},
        compiler_params=_cparams(1),
        name="sgu_%d" % chunk_len,
    )(uva, uva, w_s, b_s_t, g_a, mix)


_LOG2E = 1.4426950408889634
_SCALE2 = B_HEAD_DIM ** -0.5 * _LOG2E
_NT = (((1,), (1,)), ((), ()))


def _attn_prompt_kernel(q_ref, *rest):
    k_refs = rest[:ATT_KB]
    v_refs = rest[ATT_KB:2 * ATT_KB]
    bias_ref, gb_ref, _, o_ref, acc_ref = rest[2 * ATT_KB:]
    i = pl.program_id(0)
    for h in range(B_HEADS):
        hs = slice(h * B_HEAD_DIM, (h + 1) * B_HEAD_DIM)
        q = q_ref[:, hs]
        s = []
        for b in range(ATT_KB):
            sb = lax.dot_general(q, k_refs[b][:, hs], _NT,
                                 preferred_element_type=jnp.float32)
            sb = sb * _SCALE2 + bias_ref[h, :, b * ATT_Q:(b + 1) * ATT_Q]
            if b < ATT_KB - 1:
                sb = jnp.where(i + b >= ATT_KB - 1, sb, NEG)
            s.append(sb)
        m = functools.reduce(jnp.maximum,
                             [jnp.max(sb, axis=-1, keepdims=True) for sb in s])
        p = [jnp.exp2(sb - m) for sb in s]
        l = functools.reduce(lambda a, b: a + b,
                             [jnp.sum(pb, axis=-1, keepdims=True) for pb in p])
        o = functools.reduce(lambda a, b: a + b, [
            jnp.dot(p[b].astype(jnp.bfloat16), v_refs[b][:, hs],
                    preferred_element_type=jnp.float32) for b in range(ATT_KB)])
        acc_ref[:, hs] = o / l
    o = acc_ref[...]
    o_ref[...] = (o * _rms_scale(o) * gb_ref[...]).astype(o_ref.dtype)


def _attn_prompt(qkv16, bias, g_b, mix):
    n_blocks = T_PROMPT // ATT_Q

    def kv_spec(b, col):
        return pl.BlockSpec(
            (ATT_Q, B_WIDTH),
            lambda i: (jnp.maximum(i + b - (ATT_KB - 1), 0), col))

    in_specs = ([pl.BlockSpec((ATT_Q, B_WIDTH), lambda i: (i, 0))]
                + [kv_spec(b, 1) for b in range(ATT_KB)]
                + [kv_spec(b, 2) for b in range(ATT_KB)]
                + [pl.BlockSpec((B_HEADS, ATT_Q, ATT_KB * ATT_Q), lambda i: (0, 0, 0)),
                   pl.BlockSpec((1, B_WIDTH), lambda i: (0, 0)),
                   pl.BlockSpec(memory_space=pl.ANY)])
    n_in = len(in_specs)
    return pl.pallas_call(
        _attn_prompt_kernel,
        out_shape=jax.ShapeDtypeStruct((T_ALL, D_MODEL), jnp.bfloat16),
        grid=(n_blocks,),
        in_specs=in_specs,
        out_specs=pl.BlockSpec((ATT_Q, B_WIDTH), lambda i: (i, 1)),
        scratch_shapes=[pltpu.VMEM((ATT_Q, B_WIDTH), jnp.float32)],
        input_output_aliases={n_in - 1: 0},
        compiler_params=_cparams(1),
        name="attn_prompt",
    )(qkv16, *([qkv16] * (2 * ATT_KB)), bias, g_b, mix)


def _attn_sample_kernel(q_ref, kn_ref, vn_ref, kc_ref, vc_ref, bias_ref, gb_ref, _,
                        o_ref, acc_ref):
    n_cache = kc_ref.shape[0] // B_HEADS
    for h in range(B_HEADS):
        hs = slice(h * B_HEAD_DIM, (h + 1) * B_HEAD_DIM)
        q = q_ref[:, hs]
        kc = kc_ref[pl.ds(h, n_cache, stride=B_HEADS), :].astype(jnp.bfloat16)
        vc = vc_ref[pl.ds(h, n_cache, stride=B_HEADS), :].astype(jnp.bfloat16)
        sc = lax.dot_general(q, kc, _NT, preferred_element_type=jnp.float32)
        sc = sc * _SCALE2 + bias_ref[h, :, :n_cache]
        sn = lax.dot_general(q, kn_ref[:, hs], _NT, preferred_element_type=jnp.float32)
        sn = sn * _SCALE2 + bias_ref[h, :, n_cache:]
        m = jnp.maximum(jnp.max(sc, axis=-1, keepdims=True),
                        jnp.max(sn, axis=-1, keepdims=True))
        pc = jnp.exp2(sc - m)
        pn = jnp.exp2(sn - m)
        l = jnp.sum(pc, axis=-1, keepdims=True) + jnp.sum(pn, axis=-1, keepdims=True)
        o = (jnp.dot(pc.astype(jnp.bfloat16), vc, preferred_element_type=jnp.float32)
             + jnp.dot(pn.astype(jnp.bfloat16), vn_ref[:, hs],
                       preferred_element_type=jnp.float32))
        acc_ref[:, hs] = o / l
    o = acc_ref[...]
    o_ref[...] = (o * _rms_scale(o) * gb_ref[...]).astype(o_ref.dtype)


def _attn_sample(qkv16, cache_k, cache_v, bias, g_b, mix):
    n_cache = cache_k.shape[1] // B_HEADS
    rb0 = T_PROMPT // DEC_SEQ
    new_spec = lambda col: pl.BlockSpec((DEC_SEQ, B_WIDTH), lambda b: (rb0 + b, col))
    cache_spec = pl.BlockSpec((None, n_cache * B_HEADS, B_HEAD_DIM), lambda b: (b, 0, 0))
    return pl.pallas_call(
        _attn_sample_kernel,
        out_shape=jax.ShapeDtypeStruct((T_ALL, D_MODEL), jnp.bfloat16),
        grid=(DEC_BATCH,),
        in_specs=[new_spec(0), new_spec(1), new_spec(2), cache_spec, cache_spec,
                  pl.BlockSpec((B_HEADS, DEC_SEQ, n_cache + DEC_SEQ), lambda b: (0, 0, 0)),
                  pl.BlockSpec((1, B_WIDTH), lambda b: (0, 0)),
                  pl.BlockSpec(memory_space=pl.ANY)],
        out_specs=pl.BlockSpec((DEC_SEQ, B_WIDTH), lambda b: (rb0 + b, 1)),
        scratch_shapes=[pltpu.VMEM((DEC_SEQ, B_WIDTH), jnp.float32)],
        input_output_aliases={7: 0},
        compiler_params=_cparams(1),
        name="attn_sample",
    )(qkv16, qkv16, qkv16, cache_k, cache_v, bias, g_b, mix)


def _rel_bias_log2(table, n_q, n_k, offset):
    n = n_q + n_k - 1
    t = np.concatenate([np.arange(0, n_k), np.arange(-(n_q - 1), 0)])
    v = table[:, np.clip(offset - t, -(CHUNK - 1), REL_MAX) + (CHUNK - 1)].astype(jnp.float32)
    v = v * _LOG2E
    toeplitz = jnp.tile(v, (1, n_q))[:, :n_q * (n - 1)].reshape(-1, n_q, n - 1)
    return toeplitz[:, :, :n_k]


def _prompt_bias(table):
    n_k = ATT_KB * ATT_Q
    bias = _rel_bias_log2(table, ATT_Q, n_k, B_PAST_REACH)
    qc = np.arange(ATT_Q)[:, None] // CHUNK
    kc = np.arange(n_k)[None, :] // CHUNK
    band = (kc >= qc) & (kc <= qc + B_PAST_REACH // CHUNK)
    return jnp.where(band[None], bias, NEG)


def _split_bf16(x):
    hi = x.astype(jnp.bfloat16)
    lo = (x - hi.astype(jnp.float32)).astype(jnp.bfloat16)
    return hi, lo


def _router_kernel(xp_ref, xs_ref, hp_ref, g_ref, wr_ref, br_ref,
                   h_ref, xt_ref, eid_ref, gate_ref):
    i = pl.program_id(0)

    @pl.when(i < N_PROMPT_TILES)
    def _():
        h_ref[...] = xp_ref[...] + hp_ref[...]

    @pl.when(i >= N_PROMPT_TILES)
    def _():
        h_ref[...] = xs_ref[...] + hp_ref[...]

    h = h_ref[...]
    xt = h * _rms_scale(h) * g_ref[...]
    kd = D_MODEL // LANES
    for k in range(_pitch(kd)):
        piece = (xt[:, k * LANES:(k + 1) * LANES] if k < kd
                 else jnp.zeros((ROW_TILE, LANES), jnp.float32))
        xt_ref[pl.ds(k, ROW_TILE, stride=_pitch(kd)), :] = piece

    xh, xl = _split_bf16(xt)
    wh, wl = _split_bf16(wr_ref[...])
    dot = functools.partial(jnp.dot, preferred_element_type=jnp.float32)
    logits = dot(xh, wh) + (dot(xl, wh) + dot(xh, wl)) + br_ref[...]

    lane = lax.broadcasted_iota(jnp.int32, logits.shape, 1)
    big = jnp.int32(ROUTER_COLS)

    def masked_max(x, mask):
        return jnp.max(jnp.where(mask, x, NEG), axis=-1, keepdims=True)

    def first_lane(mask):
        return jnp.min(jnp.where(mask, lane, big), axis=-1, keepdims=True)

    gmask = lane < N_GROUPS
    gmax = masked_max(logits, gmask)
    gsum = jnp.sum(jnp.where(gmask, jnp.exp(logits - gmax), 0.0), axis=-1, keepdims=True)
    g_sel = first_lane(gmask & (logits == gmax))
    pg_top = 1.0 / gsum

    e_lo = N_GROUPS + g_sel * EXPERTS_PER_GROUP
    emask = (lane >= e_lo) & (lane < e_lo + EXPERTS_PER_GROUP)
    emax = masked_max(logits, emask)
    ex = jnp.where(emask, jnp.exp(logits - emax), 0.0)
    pe = ex / jnp.sum(ex, axis=-1, keepdims=True)
    p1 = jnp.max(pe, axis=-1, keepdims=True)
    l1 = first_lane(emask & (pe == p1))
    rest = emask & (lane != l1)
    p2 = jnp.max(jnp.where(rest, pe, -1.0), axis=-1, keepdims=True)
    l2 = first_lane(rest & (pe == p2))
    denom = p1 + p2
    g1 = pg_top * p1 / denom
    g2 = pg_top * p2 / denom
    eid_ref[...] = jnp.where(lane == 0, l1 - N_GROUPS,
                             jnp.where(lane == 1, l2 - N_GROUPS, 0))
    gate_ref[...] = jnp.where(lane == 0, g1, jnp.where(lane == 1, g2, 0.0))


def _router(xp, xs, hproj, g_ffn, w_router, b_router):
    tile = lambda cols: pl.BlockSpec((ROW_TILE, cols), lambda i: (i, 0))
    return pl.pallas_call(
        _router_kernel,
        out_shape=(jax.ShapeDtypeStruct((T_ALL, D_MODEL), jnp.float32),
                   jax.ShapeDtypeStruct((T_ALL * _pitch(D_MODEL // LANES), LANES), jnp.float32),
                   jax.ShapeDtypeStruct((T_ALL, ROUTER_COLS), jnp.int32),
                   jax.ShapeDtypeStruct((T_ALL, ROUTER_COLS), jnp.float32)),
        grid=(N_ROW_TILES,),
        in_specs=[pl.BlockSpec((ROW_TILE, D_MODEL), _prompt_tile),
                  pl.BlockSpec((ROW_TILE, D_MODEL), _sample_tile),
                  tile(D_MODEL),
                  pl.BlockSpec((1, D_MODEL), lambda i: (0, 0)),
                  pl.BlockSpec((D_MODEL, ROUTER_COLS), lambda i: (0, 0)),
                  pl.BlockSpec((1, ROUTER_COLS), lambda i: (0, 0))],
        out_specs=(tile(D_MODEL),
                   pl.BlockSpec((ROW_TILE * _pitch(D_MODEL // LANES), LANES), lambda i: (i, 0)),
                   tile(ROUTER_COLS), tile(ROUTER_COLS)),
        compiler_params=_cparams(1),
        name="router",
    )(xp, xs, hproj, g_ffn, w_router, b_router)


def _pitch(k):
    return k + 4 if k % 8 == 0 else k


def _row_copy(src_hbm, src_row, buf, slot, dst_row, sem, k):
    p = _pitch(k)
    return pltpu.make_async_copy(src_hbm.at[pl.ds(src_row * p, k)],
                                 buf.at[slot, pl.ds(dst_row * p, k)], sem.at[slot])


def _gather_rows_start(src_hbm, row_of, row0, n_rows, buf, slot, sem, k):
    def body(r, carry):
        _row_copy(src_hbm, row_of(row0 + r), buf, slot, row0 + r, sem, k).start()
        return carry
    lax.fori_loop(0, n_rows, body, 0, unroll=DMA_LOOP_UNROLL)


def _gather_rows_wait(src_hbm, row0, n_rows, buf, slot, sem, k):
    def body(r, carry):
        _row_copy(src_hbm, 0, buf, slot, row0 + r, sem, k).wait()
        return carry
    lax.fori_loop(0, n_rows, body, 0, unroll=DMA_LOOP_UNROLL)


def _rows_2d(buf, slot, row0, m, k):
    p = _pitch(k)
    return jnp.concatenate(
        [buf[slot, pl.ds(row0 * p + kk, m, stride=p), :] for kk in range(k)], axis=-1)


def _row_groups(n_rows, ch, sub, fn):
    for g in range(ch // sub):
        @pl.when(g * sub < n_rows)
        def _():
            fn(g * sub)


def _sort_rows_kernel(cm_ref, cb_ref, nv_ref, tok_ref, xt_hbm, o_ref, xbuf, sem, *, ch, sub):
    c = pl.program_id(0)
    nv = nv_ref[0]
    kd = o_ref.shape[1] // LANES
    last_tok = tok_ref.shape[0] - 1

    def start_gather(cc):
        base = cb_ref[cc]
        _row_groups(cm_ref[cc], ch, sub, lambda r0: _gather_rows_start(
            xt_hbm, lambda r: tok_ref[jnp.minimum(base + r, last_tok)], r0, sub,
            xbuf, cc % 2, sem, kd))

    @pl.when(c == 0)
    def _():
        start_gather(0)

    @pl.when(c + 1 < nv)
    def _():
        start_gather(c + 1)

    @pl.when(c < nv)
    def _():
        slot = c % 2
        _row_groups(cm_ref[c], ch, sub,
                    lambda r0: _gather_rows_wait(xt_hbm, r0, sub, xbuf, slot, sem, kd))
        for m in range(sub, ch + 1, sub):
            @pl.when(cm_ref[c] == m)
            def _():
                o_ref[:m, :] = _rows_2d(xbuf, slot, 0, m, kd).astype(o_ref.dtype)
                if m < ch:
                    o_ref[m:, :] = jnp.zeros((ch - m, o_ref.shape[1]), o_ref.dtype)


def _sort_rows(plan, xt_rows, d_model, *, ch=MOE_CH, sub=MOE_SUB):
    chunk_expert, chunk_mrows, chunk_base, n_chunks, sorted_tok = plan
    nc = chunk_expert.shape[0]
    pd = _pitch(d_model // LANES)
    grid_spec = pltpu.PrefetchScalarGridSpec(
        num_scalar_prefetch=4,
        grid=(nc,),
        in_specs=[pl.BlockSpec(memory_space=pl.ANY)],
        out_specs=pl.BlockSpec((ch, d_model),
                               lambda c, cm, cb, nv, tok: (jnp.minimum(c, nv[0] - 1), 0)),
        scratch_shapes=[pltpu.VMEM((2, ch * pd, LANES), jnp.float32),
                        pltpu.SemaphoreType.DMA((2,))],
    )
    return pl.pallas_call(
        functools.partial(_sort_rows_kernel, ch=ch, sub=sub),
        out_shape=jax.ShapeDtypeStruct((nc * ch, d_model), jnp.bfloat16),
        grid_spec=grid_spec,
        compiler_params=_cparams(1),
        name="sort_rows",
    )(chunk_mrows, chunk_base, n_chunks, sorted_tok, xt_rows)


def _expert_kernel(ce_ref, cm_ref, nv_ref, x_ref, wg_ref, wu_ref, wd_ref, o_ref, acc_ref,
                   *, ch, sub, slab):
    c = pl.program_id(0)
    j = pl.program_id(1)
    n_f = pl.num_programs(1)
    d_model, tf = wg_ref.shape
    kd = d_model // LANES
    pd = _pitch(kd)

    def dot16(a, w):
        return jnp.dot(a, w.astype(jnp.bfloat16), preferred_element_type=jnp.float32)

    @pl.when(c < nv_ref[0])
    def _():
        for m in range(sub, ch + 1, sub):
            @pl.when(cm_ref[c] == m)
            def _():
                x = x_ref[:m, :]
                hg = jnp.zeros((m, tf), jnp.float32)
                hu = jnp.zeros((m, tf), jnp.float32)
                for s in range(d_model // slab):
                    ks = slice(s * slab, (s + 1) * slab)
                    hg = hg + dot16(x[:, ks], wg_ref[ks, :])
                    hu = hu + dot16(x[:, ks], wu_ref[ks, :])
                hh = (jax.nn.silu(hg) * hu).astype(jnp.bfloat16)
                for s in range(d_model // slab):
                    ns = slice(s * slab, (s + 1) * slab)
                    y = dot16(hh, wd_ref[:, ns])

                    @pl.when(j == 0)
                    def _():
                        acc_ref[:m, ns] = y

                    @pl.when(j > 0)
                    def _():
                        acc_ref[:m, ns] += y

                @pl.when(j == n_f - 1)
                def _():
                    o_ref[...] = jnp.zeros(o_ref.shape, o_ref.dtype)
                    for k in range(kd):
                        o_ref[pl.ds(k, m, stride=pd), :] = acc_ref[:m, k * LANES:(k + 1) * LANES]


def _experts(plan, x_sorted, w_gate, w_up, w_down, *,
             ch=MOE_CH, sub=MOE_SUB, tf=MOE_TF, slab=MOE_SLAB):
    chunk_expert, chunk_mrows, _, n_chunks, _ = plan
    nc = chunk_expert.shape[0]
    d_model, d_expert = w_gate.shape[1], w_gate.shape[2]
    nf = d_expert // tf
    pd = _pitch(d_model // LANES)

    def chunk_of(c, nv):
        return jnp.minimum(c, nv[0] - 1)

    def ftile_of(c, j, nv):
        return jnp.where(c < nv[0], j, nf - 1)

    grid_spec = pltpu.PrefetchScalarGridSpec(
        num_scalar_prefetch=3,
        grid=(nc, nf),
        in_specs=[
            pl.BlockSpec((ch, d_model), lambda c, j, ce, cm, nv: (chunk_of(c, nv), 0)),
            pl.BlockSpec((None, d_model, tf),
                         lambda c, j, ce, cm, nv: (ce[chunk_of(c, nv)], 0, ftile_of(c, j, nv))),
            pl.BlockSpec((None, d_model, tf),
                         lambda c, j, ce, cm, nv: (ce[chunk_of(c, nv)], 0, ftile_of(c, j, nv))),
            pl.BlockSpec((None, tf, d_model),
                         lambda c, j, ce, cm, nv: (ce[chunk_of(c, nv)], ftile_of(c, j, nv), 0)),
        ],
        out_specs=pl.BlockSpec((ch * pd, LANES), lambda c, j, ce, cm, nv: (chunk_of(c, nv), 0)),
        scratch_shapes=[pltpu.VMEM((ch, d_model), jnp.float32)],
    )
    return pl.pallas_call(
        functools.partial(_expert_kernel, ch=ch, sub=sub, slab=slab),
        out_shape=jax.ShapeDtypeStruct((nc * ch * pd, LANES), jnp.float32),
        grid_spec=grid_spec,
        compiler_params=_cparams(2),
        name="experts",
    )(chunk_expert, chunk_mrows, n_chunks, x_sorted, w_gate, w_up, w_down)


def _moe_plan(eid, *, ch=MOE_CH, sub=MOE_SUB, nc=MOE_NC, n_experts=N_EXPERTS):
    i32 = jnp.int32
    flat_e = eid.reshape(-1)
    order = jnp.argsort(flat_e, stable=True).astype(i32)
    rank = jnp.argsort(order).astype(i32)
    onehot_e = flat_e[:, None] == jnp.arange(n_experts, dtype=i32)[None, :]
    counts = jnp.sum(onehot_e, axis=0, dtype=i32)
    start = jnp.cumsum(counts) - counts
    nchunk = (counts + ch - 1) // ch
    cend = jnp.cumsum(nchunk)
    cstart = cend - nchunk
    pos = rank + jnp.sum(jnp.where(onehot_e, (cstart * ch - start)[None, :], 0), axis=1)
    n_chunks = cend[-1:]
    cidx = jnp.arange(nc, dtype=i32)
    chunk_expert = jnp.minimum(jnp.sum(cend[None, :] <= cidx[:, None], axis=1, dtype=i32),
                               n_experts - 1)
    onehot_c = chunk_expert[:, None] == jnp.arange(n_experts, dtype=i32)[None, :]
    lookup = lambda table: jnp.sum(jnp.where(onehot_c, table[None, :], 0), axis=1)
    offset = (cidx - lookup(cstart)) * ch
    chunk_rows = jnp.clip(lookup(counts) - offset, 0, ch)
    chunk_rows = jnp.where(cidx < n_chunks[0], chunk_rows, 0)
    chunk_mrows = (chunk_rows + sub - 1) // sub * sub
    chunk_base = lookup(start) + offset
    return (chunk_expert, chunk_mrows.astype(i32), chunk_base.astype(i32),
            n_chunks.astype(i32), order // 2), pos


def _combine_kernel(pos_ref, posn_ref, h_ref, gate_ref, g_ref, y_hbm, op_ref, os_ref,
                    ybuf, sem, *, n_prompt_tiles):
    i = pl.program_id(0)
    n = pl.num_programs(0)
    tile = h_ref.shape[0]
    n_rows = 2 * tile
    kd = h_ref.shape[1] // LANES

    @pl.when(i == 0)
    def _():
        _gather_rows_start(y_hbm, lambda r: pos_ref[0, 0, r], 0, n_rows, ybuf, 0, sem, kd)

    @pl.when(i + 1 < n)
    def _():
        _gather_rows_start(y_hbm, lambda r: posn_ref[0, 0, r], 0, n_rows, ybuf, (i + 1) % 2,
                           sem, kd)

    _gather_rows_wait(y_hbm, 0, n_rows, ybuf, i % 2, sem, kd)
    slot = i % 2
    x = (h_ref[...] + gate_ref[:, 0:1] * _rows_2d(ybuf, slot, 0, tile, kd)
         + gate_ref[:, 1:2] * _rows_2d(ybuf, slot, tile, tile, kd))
    y = x * _rms_scale(x) * g_ref[...]

    @pl.when(i < n_prompt_tiles)
    def _():
        op_ref[...] = y

    @pl.when(i >= n_prompt_tiles)
    def _():
        os_ref[...] = y


def _combine(h, y_rows, pos, gates, g_final, *, t_prompt=T_PROMPT, tile=COMBINE_TILE):
    t_all, d_model = h.shape
    n_tiles = t_all // tile
    n_prompt_tiles = t_prompt // tile
    idx = pos.reshape(n_tiles, tile, 2).transpose(0, 2, 1).reshape(n_tiles, 1, 2 * tile)
    smem = lambda f: pl.BlockSpec((1, 1, 2 * tile), f, memory_space=pltpu.SMEM)
    return pl.pallas_call(
        functools.partial(_combine_kernel, n_prompt_tiles=n_prompt_tiles),
        out_shape=(jax.ShapeDtypeStruct((t_prompt, d_model), jnp.float32),
                   jax.ShapeDtypeStruct((t_all - t_prompt, d_model), jnp.float32)),
        grid=(n_tiles,),
        in_specs=[smem(lambda i: (i, 0, 0)),
                  smem(lambda i: (jnp.minimum(i + 1, n_tiles - 1), 0, 0)),
                  pl.BlockSpec((tile, d_model), lambda i: (i, 0)),
                  pl.BlockSpec((tile, gates.shape[1]), lambda i: (i, 0)),
                  pl.BlockSpec((1, d_model), lambda i: (0, 0)),
                  pl.BlockSpec(memory_space=pl.ANY)],
        out_specs=(pl.BlockSpec((tile, d_model),
                                lambda i: (jnp.minimum(i, n_prompt_tiles - 1), 0)),
                   pl.BlockSpec((tile, d_model),
                                lambda i: (jnp.maximum(i - n_prompt_tiles, 0), 0))),
        scratch_shapes=[pltpu.VMEM((2, 2 * tile * _pitch(d_model // LANES), LANES),
                                   jnp.float32),
                        pltpu.SemaphoreType.DMA((2,))],
        compiler_params=_cparams(1),
        name="combine",
    )(idx, idx, h, gates, g_final, y_rows)


def kernel(x_prompt, x_sample, cache_b_k, cache_b_v, g_mix, w_in, sgu_v_gain, sgu_w, sgu_b,
           rel_bias, g_out_a, g_out_b, w_out, g_ffn, w_router_group, b_router_group,
           w_router_expert, b_router_expert, w_gate, w_up, w_down, g_final):
    assert w_in.shape[0] == 1, "one layer"
    xp = x_prompt.reshape(T_PROMPT, D_MODEL)
    xs = x_sample.reshape(T_SAMPLE, D_MODEL)
    row = lambda v: v.reshape(1, -1)

    xn = _prenorm(xp, xs, row(g_mix[0]))
    w_in16 = w_in[0].astype(jnp.bfloat16)
    uva = _proj_uva(xn, w_in16, row(sgu_v_gain[0]))
    qkv16, qkv32 = _proj_qkv(xn, w_in16)

    w_s, b_s = sgu_w[0], sgu_b[0]
    mix = _sgu(uva, w_s, b_s.T, row(g_out_a[0]), xn,
               chunk_len=A_CHUNK, n_chunks=SGU_CHUNKS_PER_STEP, row0=0,
               n_steps=T_PROMPT // (A_CHUNK * SGU_CHUNKS_PER_STEP))
    mix = _sgu(uva, w_s[:, :DEC_SEQ, :DEC_SEQ], b_s[:, :DEC_SEQ].T, row(g_out_a[0]), mix,
               chunk_len=DEC_SEQ, n_chunks=DEC_BATCH, row0=T_PROMPT, n_steps=1)

    table = rel_bias[0]
    n_cache = cache_b_k.shape[2]
    mix = _attn_prompt(qkv16, _prompt_bias(table), row(g_out_b[0]), mix)
    bias_s = _rel_bias_log2(table, DEC_SEQ, n_cache + DEC_SEQ, n_cache)
    mix = _attn_sample(qkv16, cache_b_k[0].reshape(DEC_BATCH, n_cache * B_HEADS, B_HEAD_DIM),
                       cache_b_v[0].reshape(DEC_BATCH, n_cache * B_HEADS, B_HEAD_DIM),
                       bias_s, row(g_out_b[0]), mix)

    hproj = _proj_out(mix, w_out[0].astype(jnp.bfloat16))

    pad = ROUTER_COLS - N_GROUPS - N_EXPERTS
    w_router = jnp.pad(jnp.concatenate([w_router_group[0], w_router_expert[0]], axis=1),
                       ((0, 0), (0, pad)))
    b_router = jnp.pad(jnp.concatenate([b_router_group[0], b_router_expert[0]]), (0, pad))
    h, xt_rows, eid, gates = _router(xp, xs, hproj, row(g_ffn[0]), w_router, row(b_router))

    plan, pos = _moe_plan(eid[:, :2])
    x_sorted = _sort_rows(plan, xt_rows, D_MODEL)
    y_rows = _experts(plan, x_sorted, w_gate[0], w_up[0], w_down[0])
    y_prompt, y_sample = _combine(h, y_rows, pos, gates, row(g_final))

    keep = min(B_PAST_REACH, SEQ)
    kcols = slice(B_WIDTH, 2 * B_WIDTH)
    vcols = slice(2 * B_WIDTH, 3 * B_WIDTH)
    tail = slice(T_PROMPT - keep, T_PROMPT)
    new = slice(T_PROMPT, T_ALL)
    heads = lambda t, lead: t.reshape(*lead, B_HEADS, B_HEAD_DIM)
    return (y_prompt.reshape(1, SEQ, D_MODEL),
            y_sample.reshape(DEC_BATCH, DEC_SEQ, D_MODEL),
            heads(qkv32[tail, kcols], (1, 1, keep)),
            heads(qkv32[tail, vcols], (1, 1, keep)),
            heads(qkv32[new, kcols], (1, DEC_BATCH, DEC_SEQ)),
            heads(qkv32[new, vcols], (1, DEC_BATCH, DEC_SEQ)),
            uva[new, A_WIDTH:].reshape(1, DEC_BATCH, DEC_SEQ, A_WIDTH))
```

```python
import functools

import numpy as np
import jax
import jax.numpy as jnp
from jax import lax
from jax.experimental import pallas as pl
from jax.experimental.pallas import tpu as pltpu

D_MODEL = 4096
SEQ = 8192
DEC_BATCH = 8
DEC_SEQ = 32
T_PROMPT = SEQ
T_SAMPLE = DEC_BATCH * DEC_SEQ
T_ALL = T_PROMPT + T_SAMPLE

CHUNK = 64
A_WIDTH = D_MODEL // 2
A_GROUPS = 4
A_GROUP_DIM = A_WIDTH // A_GROUPS
A_CHUNK = 128
B_WIDTH = D_MODEL - A_WIDTH
B_HEAD_DIM = 128
B_HEADS = B_WIDTH // B_HEAD_DIM
B_PAST_REACH = 8 * CHUNK
REL_MAX = 256
N_GROUPS = 8
EXPERTS_PER_GROUP = 8
N_EXPERTS = N_GROUPS * EXPERTS_PER_GROUP
D_EXPERT = D_MODEL // 4
EPS = 1e-6
NEG = -1e30

LANES = 128
VMEM_LIMIT_BYTES = 56 * 1024 * 1024
ROW_TILE = 256
COMBINE_TILE = 128
MM_TM = 768
MM_TN = 1024
ATT_Q = 256
ATT_KB = B_PAST_REACH // ATT_Q + 1
SGU_CHUNKS_PER_STEP = 4
MOE_CH = 384
MOE_SUB = 96
MOE_TF = 256
MOE_SLAB = 1024
MOE_NC = (2 * T_ALL) // MOE_CH + N_EXPERTS
ROUTER_COLS = 128
DMA_LOOP_UNROLL = 8

assert T_ALL % MM_TM == 0 and T_PROMPT % ROW_TILE == 0 and T_SAMPLE == ROW_TILE


def _cparams(n_axes):
    return pltpu.CompilerParams(
        dimension_semantics=("arbitrary",) * n_axes,
        vmem_limit_bytes=VMEM_LIMIT_BYTES)


def _rms_scale(x):
    return lax.rsqrt(jnp.mean(x * x, axis=-1, keepdims=True) + EPS)


N_PROMPT_TILES = T_PROMPT // ROW_TILE
N_ROW_TILES = T_ALL // ROW_TILE


def _prenorm_kernel(xp_ref, xs_ref, g_ref, o_ref):
    i = pl.program_id(0)

    def norm(x):
        return (x * _rms_scale(x) * g_ref[...]).astype(o_ref.dtype)

    @pl.when(i < N_PROMPT_TILES)
    def _():
        o_ref[...] = norm(xp_ref[...])

    @pl.when(i >= N_PROMPT_TILES)
    def _():
        o_ref[...] = norm(xs_ref[...])


def _prompt_tile(i):
    return (jnp.minimum(i, N_PROMPT_TILES - 1), 0)


def _sample_tile(i):
    return (jnp.maximum(i - N_PROMPT_TILES, 0), 0)


def _prenorm(xp, xs, g):
    return pl.pallas_call(
        _prenorm_kernel,
        out_shape=jax.ShapeDtypeStruct((T_ALL, D_MODEL), jnp.bfloat16),
        grid=(N_ROW_TILES,),
        in_specs=[pl.BlockSpec((ROW_TILE, D_MODEL), _prompt_tile),
                  pl.BlockSpec((ROW_TILE, D_MODEL), _sample_tile),
                  pl.BlockSpec((1, D_MODEL), lambda i: (0, 0))],
        out_specs=pl.BlockSpec((ROW_TILE, D_MODEL), lambda i: (i, 0)),
        compiler_params=_cparams(1),
        name="prenorm",
    )(xp, xs, g)


def _uva_kernel(x_ref, w_ref, vg_ref, o_ref):
    j = pl.program_id(0)
    acc = jnp.dot(x_ref[...], w_ref[...], preferred_element_type=jnp.float32)
    g = jax.nn.gelu(acc)

    @pl.when(j < A_WIDTH // MM_TN)
    def _():
        o_ref[...] = g

    @pl.when(j >= A_WIDTH // MM_TN)
    def _():
        for k in range(MM_TN // A_GROUP_DIM):
            cols = slice(k * A_GROUP_DIM, (k + 1) * A_GROUP_DIM)
            gg = g[:, cols]
            o_ref[:, cols] = gg * _rms_scale(gg) * vg_ref[:, cols]


def _qkv_kernel(x_ref, w_ref, o16_ref, o32_ref):
    acc = jnp.dot(x_ref[...], w_ref[...], preferred_element_type=jnp.float32)
    o16_ref[...] = acc.astype(o16_ref.dtype)
    o32_ref[...] = acc


def _plain_mm_kernel(x_ref, w_ref, o_ref):
    o_ref[...] = jnp.dot(x_ref[...], w_ref[...], preferred_element_type=jnp.float32)


def _proj_uva(xn, w_in16, v_gain):
    n_va_tiles = A_WIDTH // MM_TN
    return pl.pallas_call(
        _uva_kernel,
        out_shape=jax.ShapeDtypeStruct((T_ALL, 2 * A_WIDTH), jnp.float32),
        grid=(2 * A_WIDTH // MM_TN, T_ALL // MM_TM),
        in_specs=[pl.BlockSpec((MM_TM, D_MODEL), lambda j, i: (i, 0)),
                  pl.BlockSpec((D_MODEL, MM_TN), lambda j, i: (0, j)),
                  pl.BlockSpec((1, MM_TN), lambda j, i: (0, jnp.maximum(j - n_va_tiles, 0)))],
        out_specs=pl.BlockSpec((MM_TM, MM_TN), lambda j, i: (i, j)),
        compiler_params=_cparams(2),
        name="proj_uva",
    )(xn, w_in16, v_gain)


def _proj_qkv(xn, w_in16):
    col0 = 2 * A_WIDTH // MM_TN
    n_cols = 3 * B_WIDTH
    shape = (T_ALL, n_cols)
    return pl.pallas_call(
        _qkv_kernel,
        out_shape=(jax.ShapeDtypeStruct(shape, jnp.bfloat16),
                   jax.ShapeDtypeStruct(shape, jnp.float32)),
        grid=(n_cols // MM_TN, T_ALL // MM_TM),
        in_specs=[pl.BlockSpec((MM_TM, D_MODEL), lambda j, i: (i, 0)),
                  pl.BlockSpec((D_MODEL, MM_TN), lambda j, i: (0, col0 + j))],
        out_specs=(pl.BlockSpec((MM_TM, MM_TN), lambda j, i: (i, j)),
                   pl.BlockSpec((MM_TM, MM_TN), lambda j, i: (i, j))),
        compiler_params=_cparams(2),
        name="proj_qkv",
    )(xn, w_in16)


def _proj_out(mix, w_out16):
    return pl.pallas_call(
        _plain_mm_kernel,
        out_shape=jax.ShapeDtypeStruct((T_ALL, D_MODEL), jnp.float32),
        grid=(D_MODEL // MM_TN, T_ALL // MM_TM),
        in_specs=[pl.BlockSpec((MM_TM, D_MODEL), lambda j, i: (i, 0)),
                  pl.BlockSpec((D_MODEL, MM_TN), lambda j, i: (0, j))],
        out_specs=pl.BlockSpec((MM_TM, MM_TN), lambda j, i: (i, j)),
        compiler_params=_cparams(2),
        name="proj_out",
    )(mix, w_out16)


def _sgu_kernel(u_ref, v_ref, w_ref, b_ref, ga_ref, *rest, chunk_len, n_chunks):
    o_ref = rest[-1]
    L = chunk_len
    row = lax.broadcasted_iota(jnp.int32, (L, L), 0)
    col = lax.broadcasted_iota(jnp.int32, (L, L), 1)
    w16 = [jnp.where(col <= row, w_ref[g], 0.0).astype(jnp.bfloat16)
           for g in range(A_GROUPS)]
    for c in range(n_chunks):
        rows = slice(c * L, (c + 1) * L)
        parts = []
        ss = jnp.zeros((L, 1), jnp.float32)
        for g in range(A_GROUPS):
            cols = slice(g * A_GROUP_DIM, (g + 1) * A_GROUP_DIM)
            f = jnp.dot(w16[g], v_ref[rows, cols].astype(jnp.bfloat16),
                        preferred_element_type=jnp.float32) + b_ref[:, g:g + 1]
            a = u_ref[rows, cols] * f
            parts.append(a)
            ss = ss + jnp.sum(a * a, axis=-1, keepdims=True)
        r = lax.rsqrt(ss * (1.0 / A_WIDTH) + EPS)
        for g in range(A_GROUPS):
            cols = slice(g * A_GROUP_DIM, (g + 1) * A_GROUP_DIM)
            o_ref[rows, cols] = (parts[g] * r * ga_ref[:, cols]).astype(o_ref.dtype)


def _sgu(uva, w_s, b_s_t, g_a, mix, *, chunk_len, n_chunks, row0, n_steps):
    rows = chunk_len * n_chunks
    rb0 = row0 // rows
    kernel = functools.partial(_sgu_kernel, chunk_len=chunk_len, n_chunks=n_chunks)
    return pl.pallas_call(
        kernel,
        out_shape=jax.ShapeDtypeStruct((T_ALL, D_MODEL), jnp.bfloat16),
        grid=(n_steps,),
        in_specs=[pl.BlockSpec((rows, A_WIDTH), lambda i: (rb0 + i, 0)),
                  pl.BlockSpec((rows, A_WIDTH), lambda i: (rb0 + i, 1)),
                  pl.BlockSpec((A_GROUPS, chunk_len, chunk_len), lambda i: (0, 0, 0)),
                  pl.BlockSpec((chunk_len, A_GROUPS), lambda i: (0, 0)),
                  pl.BlockSpec((1, A_WIDTH), lambda i: (0, 0)),
                  pl.BlockSpec(memory_space=pl.ANY)],
        out_specs=pl.BlockSpec((rows, A_WIDTH), lambda i: (rb0 + i, 0)),
        input_output_aliases={5: 0},
        compiler_params=_cparams(1),
        name="sgu_%d" % chunk_len,
    )(uva, uva, w_s, b_s_t, g_a, mix)


_LOG2E = 1.4426950408889634
_SCALE2 = B_HEAD_DIM ** -0.5 * _LOG2E
_NT = (((1,), (1,)), ((), ()))


def _attn_prompt_kernel(q_ref, *rest):
    k_refs = rest[:ATT_KB]
    v_refs = rest[ATT_KB:2 * ATT_KB]
    bias_ref, gb_ref, _, o_ref, acc_ref = rest[2 * ATT_KB:]
    i = pl.program_id(0)
    for h in range(B_HEADS):
        hs = slice(h * B_HEAD_DIM, (h + 1) * B_HEAD_DIM)
        q = q_ref[:, hs]
        s = []
        for b in range(ATT_KB):
            sb = lax.dot_general(q, k_refs[b][:, hs], _NT,
                                 preferred_element_type=jnp.float32)
            sb = sb * _SCALE2 + bias_ref[h, :, b * ATT_Q:(b + 1) * ATT_Q]
            if b < ATT_KB - 1:
                sb = jnp.where(i + b >= ATT_KB - 1, sb, NEG)
            s.append(sb)
        m = functools.reduce(jnp.maximum,
                             [jnp.max(sb, axis=-1, keepdims=True) for sb in s])
        p = [jnp.exp2(sb - m) for sb in s]
        l = functools.reduce(lambda a, b: a + b,
                             [jnp.sum(pb, axis=-1, keepdims=True) for pb in p])
        o = functools.reduce(lambda a, b: a + b, [
            jnp.dot(p[b].astype(jnp.bfloat16), v_refs[b][:, hs],
                    preferred_element_type=jnp.float32) for b in range(ATT_KB)])
        acc_ref[:, hs] = o / l
    o = acc_ref[...]
    o_ref[...] = (o * _rms_scale(o) * gb_ref[...]).astype(o_ref.dtype)


def _attn_prompt(qkv16, bias, g_b, mix):
    n_blocks = T_PROMPT // ATT_Q

    def kv_spec(b, col):
        return pl.BlockSpec(
            (ATT_Q, B_WIDTH),
            lambda i: (jnp.maximum(i + b - (ATT_KB - 1), 0), col))

    in_specs = ([pl.BlockSpec((ATT_Q, B_WIDTH), lambda i: (i, 0))]
                + [kv_spec(b, 1) for b in range(ATT_KB)]
                + [kv_spec(b, 2) for b in range(ATT_KB)]
                + [pl.BlockSpec((B_HEADS, ATT_Q, ATT_KB * ATT_Q), lambda i: (0, 0, 0)),
                   pl.BlockSpec((1, B_WIDTH), lambda i: (0, 0)),
                   pl.BlockSpec(memory_space=pl.ANY)])
    n_in = len(in_specs)
    return pl.pallas_call(
        _attn_prompt_kernel,
        out_shape=jax.ShapeDtypeStruct((T_ALL, D_MODEL), jnp.bfloat16),
        grid=(n_blocks,),
        in_specs=in_specs,
        out_specs=pl.BlockSpec((ATT_Q, B_WIDTH), lambda i: (i, 1)),
        scratch_shapes=[pltpu.VMEM((ATT_Q, B_WIDTH), jnp.float32)],
        input_output_aliases={n_in - 1: 0},
        compiler_params=_cparams(1),
        name="attn_prompt",
    )(qkv16, *([qkv16] * (2 * ATT_KB)), bias, g_b, mix)


def _attn_sample_kernel(q_ref, kn_ref, vn_ref, kc_ref, vc_ref, bias_ref, gb_ref, _,
                        o_ref, acc_ref):
    n_cache = kc_ref.shape[0] // B_HEADS
    for h in range(B_HEADS):
        hs = slice(h * B_HEAD_DIM, (h + 1) * B_HEAD_DIM)
        q = q_ref[:, hs]
        kc = kc_ref[pl.ds(h, n_cache, stride=B_HEADS), :].astype(jnp.bfloat16)
        vc = vc_ref[pl.ds(h, n_cache, stride=B_HEADS), :].astype(jnp.bfloat16)
        sc = lax.dot_general(q, kc, _NT, preferred_element_type=jnp.float32)
        sc = sc * _SCALE2 + bias_ref[h, :, :n_cache]
        sn = lax.dot_general(q, kn_ref[:, hs], _NT, preferred_element_type=jnp.float32)
        sn = sn * _SCALE2 + bias_ref[h, :, n_cache:]
        m = jnp.maximum(jnp.max(sc, axis=-1, keepdims=True),
                        jnp.max(sn, axis=-1, keepdims=True))
        pc = jnp.exp2(sc - m)
        pn = jnp.exp2(sn - m)
        l = jnp.sum(pc, axis=-1, keepdims=True) + jnp.sum(pn, axis=-1, keepdims=True)
        o = (jnp.dot(pc.astype(jnp.bfloat16), vc, preferred_element_type=jnp.float32)
             + jnp.dot(pn.astype(jnp.bfloat16), vn_ref[:, hs],
                       preferred_element_type=jnp.float32))
        acc_ref[:, hs] = o / l
    o = acc_ref[...]
    o_ref[...] = (o * _rms_scale(o) * gb_ref[...]).astype(o_ref.dtype)


def _attn_sample(qkv16, cache_k, cache_v, bias, g_b, mix):
    n_cache = cache_k.shape[1] // B_HEADS
    rb0 = T_PROMPT // DEC_SEQ
    new_spec = lambda col: pl.BlockSpec((DEC_SEQ, B_WIDTH), lambda b: (rb0 + b, col))
    cache_spec = pl.BlockSpec((None, n_cache * B_HEADS, B_HEAD_DIM), lambda b: (b, 0, 0))
    return pl.pallas_call(
        _attn_sample_kernel,
        out_shape=jax.ShapeDtypeStruct((T_ALL, D_MODEL), jnp.bfloat16),
        grid=(DEC_BATCH,),
        in_specs=[new_spec(0), new_spec(1), new_spec(2), cache_spec, cache_spec,
                  pl.BlockSpec((B_HEADS, DEC_SEQ, n_cache + DEC_SEQ), lambda b: (0, 0, 0)),
                  pl.BlockSpec((1, B_WIDTH), lambda b: (0, 0)),
                  pl.BlockSpec(memory_space=pl.ANY)],
        out_specs=pl.BlockSpec((DEC_SEQ, B_WIDTH), lambda b: (rb0 + b, 1)),
        scratch_shapes=[pltpu.VMEM((DEC_SEQ, B_WIDTH), jnp.float32)],
        input_output_aliases={7: 0},
        compiler_params=_cparams(1),
        name="attn_sample",
    )(qkv16, qkv16, qkv16, cache_k, cache_v, bias, g_b, mix)


def _rel_bias_log2(table, n_q, n_k, offset):
    n = n_q + n_k - 1
    t = np.concatenate([np.arange(0, n_k), np.arange(-(n_q - 1), 0)])
    v = table[:, np.clip(offset - t, -(CHUNK - 1), REL_MAX) + (CHUNK - 1)].astype(jnp.float32)
    v = v * _LOG2E
    toeplitz = jnp.tile(v, (1, n_q))[:, :n_q * (n - 1)].reshape(-1, n_q, n - 1)
    return toeplitz[:, :, :n_k]


def _prompt_bias(table):
    n_k = ATT_KB * ATT_Q
    bias = _rel_bias_log2(table, ATT_Q, n_k, B_PAST_REACH)
    qc = np.arange(ATT_Q)[:, None] // CHUNK
    kc = np.arange(n_k)[None, :] // CHUNK
    band = (kc >= qc) & (kc <= qc + B_PAST_REACH // CHUNK)
    return jnp.where(band[None], bias, NEG)


def _split_bf16(x):
    hi = x.astype(jnp.bfloat16)
    lo = (x - hi.astype(jnp.float32)).astype(jnp.bfloat16)
    return hi, lo


def _router_kernel(xp_ref, xs_ref, hp_ref, g_ref, wr_ref, br_ref,
                   h_ref, xt_ref, eid_ref, gate_ref):
    i = pl.program_id(0)

    @pl.when(i < N_PROMPT_TILES)
    def _():
        h_ref[...] = xp_ref[...] + hp_ref[...]

    @pl.when(i >= N_PROMPT_TILES)
    def _():
        h_ref[...] = xs_ref[...] + hp_ref[...]

    h = h_ref[...]
    xt = h * _rms_scale(h) * g_ref[...]
    kd = D_MODEL // LANES
    for k in range(_pitch(kd)):
        piece = (xt[:, k * LANES:(k + 1) * LANES] if k < kd
                 else jnp.zeros((ROW_TILE, LANES), jnp.float32))
        xt_ref[pl.ds(k, ROW_TILE, stride=_pitch(kd)), :] = piece

    xh, xl = _split_bf16(xt)
    wh, wl = _split_bf16(wr_ref[...])
    dot = functools.partial(jnp.dot, preferred_element_type=jnp.float32)
    logits = dot(xh, wh) + (dot(xl, wh) + dot(xh, wl)) + br_ref[...]

    lane = lax.broadcasted_iota(jnp.int32, logits.shape, 1)
    big = jnp.int32(ROUTER_COLS)

    def masked_max(x, mask):
        return jnp.max(jnp.where(mask, x, NEG), axis=-1, keepdims=True)

    def first_lane(mask):
        return jnp.min(jnp.where(mask, lane, big), axis=-1, keepdims=True)

    gmask = lane < N_GROUPS
    gmax = masked_max(logits, gmask)
    gsum = jnp.sum(jnp.where(gmask, jnp.exp(logits - gmax), 0.0), axis=-1, keepdims=True)
    g_sel = first_lane(gmask & (logits == gmax))
    pg_top = 1.0 / gsum

    e_lo = N_GROUPS + g_sel * EXPERTS_PER_GROUP
    emask = (lane >= e_lo) & (lane < e_lo + EXPERTS_PER_GROUP)
    emax = masked_max(logits, emask)
    ex = jnp.where(emask, jnp.exp(logits - emax), 0.0)
    pe = ex / jnp.sum(ex, axis=-1, keepdims=True)
    p1 = jnp.max(pe, axis=-1, keepdims=True)
    l1 = first_lane(emask & (pe == p1))
    rest = emask & (lane != l1)
    p2 = jnp.max(jnp.where(rest, pe, -1.0), axis=-1, keepdims=True)
    l2 = first_lane(rest & (pe == p2))
    denom = p1 + p2
    g1 = pg_top * p1 / denom
    g2 = pg_top * p2 / denom
    eid_ref[...] = jnp.where(lane == 0, l1 - N_GROUPS,
                             jnp.where(lane == 1, l2 - N_GROUPS, 0))
    gate_ref[...] = jnp.where(lane == 0, g1, jnp.where(lane == 1, g2, 0.0))


def _router(xp, xs, hproj, g_ffn, w_router, b_router):
    tile = lambda cols: pl.BlockSpec((ROW_TILE, cols), lambda i: (i, 0))
    return pl.pallas_call(
        _router_kernel,
        out_shape=(jax.ShapeDtypeStruct((T_ALL, D_MODEL), jnp.float32),
                   jax.ShapeDtypeStruct((T_ALL * _pitch(D_MODEL // LANES), LANES), jnp.float32),
                   jax.ShapeDtypeStruct((T_ALL, ROUTER_COLS), jnp.int32),
                   jax.ShapeDtypeStruct((T_ALL, ROUTER_COLS), jnp.float32)),
        grid=(N_ROW_TILES,),
        in_specs=[pl.BlockSpec((ROW_TILE, D_MODEL), _prompt_tile),
                  pl.BlockSpec((ROW_TILE, D_MODEL), _sample_tile),
                  tile(D_MODEL),
                  pl.BlockSpec((1, D_MODEL), lambda i: (0, 0)),
                  pl.BlockSpec((D_MODEL, ROUTER_COLS), lambda i: (0, 0)),
                  pl.BlockSpec((1, ROUTER_COLS), lambda i: (0, 0))],
        out_specs=(tile(D_MODEL),
                   pl.BlockSpec((ROW_TILE * _pitch(D_MODEL // LANES), LANES), lambda i: (i, 0)),
                   tile(ROUTER_COLS), tile(ROUTER_COLS)),
        compiler_params=_cparams(1),
        name="router",
    )(xp, xs, hproj, g_ffn, w_router, b_router)


def _pitch(k):
    return k + 4 if k % 8 == 0 else k


def _row_copy(src_hbm, src_row, buf, slot, dst_row, sem, k):
    p = _pitch(k)
    return pltpu.make_async_copy(src_hbm.at[pl.ds(src_row * p, k)],
                                 buf.at[slot, pl.ds(dst_row * p, k)], sem.at[slot])


def _gather_rows_start(src_hbm, row_of, row0, n_rows, buf, slot, sem, k):
    def body(r, carry):
        _row_copy(src_hbm, row_of(row0 + r), buf, slot, row0 + r, sem, k).start()
        return carry
    lax.fori_loop(0, n_rows, body, 0, unroll=DMA_LOOP_UNROLL)


def _gather_rows_wait(src_hbm, row0, n_rows, buf, slot, sem, k):
    def body(r, carry):
        _row_copy(src_hbm, 0, buf, slot, row0 + r, sem, k).wait()
        return carry
    lax.fori_loop(0, n_rows, body, 0, unroll=DMA_LOOP_UNROLL)


def _rows_2d(buf, slot, row0, m, k):
    p = _pitch(k)
    return jnp.concatenate(
        [buf[slot, pl.ds(row0 * p + kk, m, stride=p), :] for kk in range(k)], axis=-1)


def _row_groups(n_rows, ch, sub, fn):
    for g in range(ch // sub):
        @pl.when(g * sub < n_rows)
        def _():
            fn(g * sub)


def _sort_rows_kernel(cm_ref, cb_ref, nv_ref, tok_ref, xt_hbm, o_ref, xbuf, sem, *, ch, sub):
    c = pl.program_id(0)
    nv = nv_ref[0]
    kd = o_ref.shape[1] // LANES
    last_tok = tok_ref.shape[0] - 1

    def start_gather(cc):
        base = cb_ref[cc]
        _row_groups(cm_ref[cc], ch, sub, lambda r0: _gather_rows_start(
            xt_hbm, lambda r: tok_ref[jnp.minimum(base + r, last_tok)], r0, sub,
            xbuf, cc % 2, sem, kd))

    @pl.when(c == 0)
    def _():
        start_gather(0)

    @pl.when(c + 1 < nv)
    def _():
        start_gather(c + 1)

    @pl.when(c < nv)
    def _():
        slot = c % 2
        _row_groups(cm_ref[c], ch, sub,
                    lambda r0: _gather_rows_wait(xt_hbm, r0, sub, xbuf, slot, sem, kd))
        for m in range(sub, ch + 1, sub):
            @pl.when(cm_ref[c] == m)
            def _():
                o_ref[:m, :] = _rows_2d(xbuf, slot, 0, m, kd).astype(o_ref.dtype)
                if m < ch:
                    o_ref[m:, :] = jnp.zeros((ch - m, o_ref.shape[1]), o_ref.dtype)


def _sort_rows(plan, xt_rows, d_model, *, ch=MOE_CH, sub=MOE_SUB):
    chunk_expert, chunk_mrows, chunk_base, n_chunks, sorted_tok = plan
    nc = chunk_expert.shape[0]
    pd = _pitch(d_model // LANES)
    grid_spec = pltpu.PrefetchScalarGridSpec(
        num_scalar_prefetch=4,
        grid=(nc,),
        in_specs=[pl.BlockSpec(memory_space=pl.ANY)],
        out_specs=pl.BlockSpec((ch, d_model),
                               lambda c, cm, cb, nv, tok: (jnp.minimum(c, nv[0] - 1), 0)),
        scratch_shapes=[pltpu.VMEM((2, ch * pd, LANES), jnp.float32),
                        pltpu.SemaphoreType.DMA((2,))],
    )
    return pl.pallas_call(
        functools.partial(_sort_rows_kernel, ch=ch, sub=sub),
        out_shape=jax.ShapeDtypeStruct((nc * ch, d_model), jnp.bfloat16),
        grid_spec=grid_spec,
        compiler_params=_cparams(1),
        name="sort_rows",
    )(chunk_mrows, chunk_base, n_chunks, sorted_tok, xt_rows)


def _expert_kernel(ce_ref, cm_ref, nv_ref, x_ref, wg_ref, wu_ref, wd_ref, o_ref,
                   *, ch, sub, slab):
    c = pl.program_id(0)
    j = pl.program_id(1)
    d_model, tf = wg_ref.shape
    kd = d_model // LANES
    pd = _pitch(kd)

    def dot16(a, w):
        return jnp.dot(a, w.astype(jnp.bfloat16), preferred_element_type=jnp.float32)

    @pl.when((c < nv_ref[0]) & (j == 0))
    def _():
        o_ref[...] = jnp.zeros(o_ref.shape, o_ref.dtype)

    @pl.when(c < nv_ref[0])
    def _():
        for m in range(sub, ch + 1, sub):
            @pl.when(cm_ref[c] == m)
            def _():
                x = x_ref[:m, :]
                hg = jnp.zeros((m, tf), jnp.float32)
                hu = jnp.zeros((m, tf), jnp.float32)
                for s in range(d_model // slab):
                    ks = slice(s * slab, (s + 1) * slab)
                    hg = hg + dot16(x[:, ks], wg_ref[ks, :])
                    hu = hu + dot16(x[:, ks], wu_ref[ks, :])
                hh = (jax.nn.silu(hg) * hu).astype(jnp.bfloat16)
                for s in range(d_model // slab):
                    y = dot16(hh, wd_ref[:, s * slab:(s + 1) * slab])
                    for kk in range(slab // LANES):
                        k = s * (slab // LANES) + kk
                        o_ref[pl.ds(k, m, stride=pd), :] += y[:, kk * LANES:(kk + 1) * LANES]


def _experts(plan, x_sorted, w_gate, w_up, w_down, *,
             ch=MOE_CH, sub=MOE_SUB, tf=MOE_TF, slab=MOE_SLAB):
    chunk_expert, chunk_mrows, _, n_chunks, _ = plan
    nc = chunk_expert.shape[0]
    d_model, d_expert = w_gate.shape[1], w_gate.shape[2]
    nf = d_expert // tf
    pd = _pitch(d_model // LANES)

    def chunk_of(c, nv):
        return jnp.minimum(c, nv[0] - 1)

    def ftile_of(c, j, nv):
        return jnp.where(c < nv[0], j, nf - 1)

    grid_spec = pltpu.PrefetchScalarGridSpec(
        num_scalar_prefetch=3,
        grid=(nc, nf),
        in_specs=[
            pl.BlockSpec((ch, d_model), lambda c, j, ce, cm, nv: (chunk_of(c, nv), 0)),
            pl.BlockSpec((None, d_model, tf),
                         lambda c, j, ce, cm, nv: (ce[chunk_of(c, nv)], 0, ftile_of(c, j, nv))),
            pl.BlockSpec((None, d_model, tf),
                         lambda c, j, ce, cm, nv: (ce[chunk_of(c, nv)], 0, ftile_of(c, j, nv))),
            pl.BlockSpec((None, tf, d_model),
                         lambda c, j, ce, cm, nv: (ce[chunk_of(c, nv)], ftile_of(c, j, nv), 0)),
        ],
        out_specs=pl.BlockSpec((ch * pd, LANES), lambda c, j, ce, cm, nv: (chunk_of(c, nv), 0)),
    )
    return pl.pallas_call(
        functools.partial(_expert_kernel, ch=ch, sub=sub, slab=slab),
        out_shape=jax.ShapeDtypeStruct((nc * ch * pd, LANES), jnp.float32),
        grid_spec=grid_spec,
        compiler_params=_cparams(2),
        name="experts",
    )(chunk_expert, chunk_mrows, n_chunks, x_sorted, w_gate, w_up, w_down)


def _moe_plan(eid, *, ch=MOE_CH, sub=MOE_SUB, nc=MOE_NC, n_experts=N_EXPERTS):
    i32 = jnp.int32
    flat_e = eid.reshape(-1)
    order = jnp.argsort(flat_e, stable=True).astype(i32)
    rank = jnp.argsort(order).astype(i32)
    onehot_e = flat_e[:, None] == jnp.arange(n_experts, dtype=i32)[None, :]
    counts = jnp.sum(onehot_e, axis=0, dtype=i32)
    start = jnp.cumsum(counts) - counts
    nchunk = (counts + ch - 1) // ch
    cend = jnp.cumsum(nchunk)
    cstart = cend - nchunk
    pos = rank + jnp.sum(jnp.where(onehot_e, (cstart * ch - start)[None, :], 0), axis=1)
    n_chunks = cend[-1:]
    cidx = jnp.arange(nc, dtype=i32)
    chunk_expert = jnp.minimum(jnp.sum(cend[None, :] <= cidx[:, None], axis=1, dtype=i32),
                               n_experts - 1)
    onehot_c = chunk_expert[:, None] == jnp.arange(n_experts, dtype=i32)[None, :]
    lookup = lambda table: jnp.sum(jnp.where(onehot_c, table[None, :], 0), axis=1)
    offset = (cidx - lookup(cstart)) * ch
    chunk_rows = jnp.clip(lookup(counts) - offset, 0, ch)
    chunk_rows = jnp.where(cidx < n_chunks[0], chunk_rows, 0)
    chunk_mrows = (chunk_rows + sub - 1) // sub * sub
    chunk_base = lookup(start) + offset
    return (chunk_expert, chunk_mrows.astype(i32), chunk_base.astype(i32),
            n_chunks.astype(i32), order // 2), pos


def _combine_kernel(pos_ref, posn_ref, h_ref, gate_ref, g_ref, y_hbm, op_ref, os_ref,
                    ybuf, sem, *, n_prompt_tiles):
    i = pl.program_id(0)
    n = pl.num_programs(0)
    tile = h_ref.shape[0]
    n_rows = 2 * tile
    kd = h_ref.shape[1] // LANES

    @pl.when(i == 0)
    def _():
        _gather_rows_start(y_hbm, lambda r: pos_ref[0, 0, r], 0, n_rows, ybuf, 0, sem, kd)

    @pl.when(i + 1 < n)
    def _():
        _gather_rows_start(y_hbm, lambda r: posn_ref[0, 0, r], 0, n_rows, ybuf, (i + 1) % 2,
                           sem, kd)

    _gather_rows_wait(y_hbm, 0, n_rows, ybuf, i % 2, sem, kd)
    slot = i % 2
    x = (h_ref[...] + gate_ref[:, 0:1] * _rows_2d(ybuf, slot, 0, tile, kd)
         + gate_ref[:, 1:2] * _rows_2d(ybuf, slot, tile, tile, kd))
    y = x * _rms_scale(x) * g_ref[...]

    @pl.when(i < n_prompt_tiles)
    def _():
        op_ref[...] = y

    @pl.when(i >= n_prompt_tiles)
    def _():
        os_ref[...] = y


def _combine(h, y_rows, pos, gates, g_final, *, t_prompt=T_PROMPT, tile=COMBINE_TILE):
    t_all, d_model = h.shape
    n_tiles = t_all // tile
    n_prompt_tiles = t_prompt // tile
    idx = pos.reshape(n_tiles, tile, 2).transpose(0, 2, 1).reshape(n_tiles, 1, 2 * tile)
    smem = lambda f: pl.BlockSpec((1, 1, 2 * tile), f, memory_space=pltpu.SMEM)
    return pl.pallas_call(
        functools.partial(_combine_kernel, n_prompt_tiles=n_prompt_tiles),
        out_shape=(jax.ShapeDtypeStruct((t_prompt, d_model), jnp.float32),
                   jax.ShapeDtypeStruct((t_all - t_prompt, d_model), jnp.float32)),
        grid=(n_tiles,),
        in_specs=[smem(lambda i: (i, 0, 0)),
                  smem(lambda i: (jnp.minimum(i + 1, n_tiles - 1), 0, 0)),
                  pl.BlockSpec((tile, d_model), lambda i: (i, 0)),
                  pl.BlockSpec((tile, gates.shape[1]), lambda i: (i, 0)),
                  pl.BlockSpec((1, d_model), lambda i: (0, 0)),
                  pl.BlockSpec(memory_space=pl.ANY)],
        out_specs=(pl.BlockSpec((tile, d_model),
                                lambda i: (jnp.minimum(i, n_prompt_tiles - 1), 0)),
                   pl.BlockSpec((tile, d_model),
                                lambda i: (jnp.maximum(i - n_prompt_tiles, 0), 0))),
        scratch_shapes=[pltpu.VMEM((2, 2 * tile * _pitch(d_model // LANES), LANES),
                                   jnp.float32),
                        pltpu.SemaphoreType.DMA((2,))],
        compiler_params=_cparams(1),
        name="combine",
    )(idx, idx, h, gates, g_final, y_rows)


def kernel(x_prompt, x_sample, cache_b_k, cache_b_v, g_mix, w_in, sgu_v_gain, sgu_w, sgu_b,
           rel_bias, g_out_a, g_out_b, w_out, g_ffn, w_router_group, b_router_group,
           w_router_expert, b_router_expert, w_gate, w_up, w_down, g_final):
    assert w_in.shape[0] == 1, "one layer"
    xp = x_prompt.reshape(T_PROMPT, D_MODEL)
    xs = x_sample.reshape(T_SAMPLE, D_MODEL)
    row = lambda v: v.reshape(1, -1)

    xn = _prenorm(xp, xs, row(g_mix[0]))
    w_in16 = w_in[0].astype(jnp.bfloat16)
    uva = _proj_uva(xn, w_in16, row(sgu_v_gain[0]))
    qkv16, qkv32 = _proj_qkv(xn, w_in16)

    w_s, b_s = sgu_w[0], sgu_b[0]
    mix = _sgu(uva, w_s, b_s.T, row(g_out_a[0]), xn,
               chunk_len=A_CHUNK, n_chunks=SGU_CHUNKS_PER_STEP, row0=0,
               n_steps=T_PROMPT // (A_CHUNK * SGU_CHUNKS_PER_STEP))
    mix = _sgu(uva, w_s[:, :DEC_SEQ, :DEC_SEQ], b_s[:, :DEC_SEQ].T, row(g_out_a[0]), mix,
               chunk_len=DEC_SEQ, n_chunks=DEC_BATCH, row0=T_PROMPT, n_steps=1)

    table = rel_bias[0]
    n_cache = cache_b_k.shape[2]
    mix = _attn_prompt(qkv16, _prompt_bias(table), row(g_out_b[0]), mix)
    bias_s = _rel_bias_log2(table, DEC_SEQ, n_cache + DEC_SEQ, n_cache)
    mix = _attn_sample(qkv16, cache_b_k[0].reshape(DEC_BATCH, n_cache * B_HEADS, B_HEAD_DIM),
                       cache_b_v[0].reshape(DEC_BATCH, n_cache * B_HEADS, B_HEAD_DIM),
                       bias_s, row(g_out_b[0]), mix)

    hproj = _proj_out(mix, w_out[0].astype(jnp.bfloat16))

    pad = ROUTER_COLS - N_GROUPS - N_EXPERTS
    w_router = jnp.pad(jnp.concatenate([w_router_group[0], w_router_expert[0]], axis=1),
                       ((0, 0), (0, pad)))
    b_router = jnp.pad(jnp.concatenate([b_router_group[0], b_router_expert[0]]), (0, pad))
    h, xt_rows, eid, gates = _router(xp, xs, hproj, row(g_ffn[0]), w_router, row(b_router))

    plan, pos = _moe_plan(eid[:, :2])
    x_sorted = _sort_rows(plan, xt_rows, D_MODEL)
    y_rows = _experts(plan, x_sorted, w_gate[0], w_up[0], w_down[0])
    y_prompt, y_sample = _combine(h, y_rows, pos, gates, row(g_final))

    keep = min(B_PAST_REACH, SEQ)
    kcols = slice(B_WIDTH, 2 * B_WIDTH)
    vcols = slice(2 * B_WIDTH, 3 * B_WIDTH)
    tail = slice(T_PROMPT - keep, T_PROMPT)
    new = slice(T_PROMPT, T_ALL)
    heads = lambda t, lead: t.reshape(*lead, B_HEADS, B_HEAD_DIM)
    return (y_prompt.reshape(1, SEQ, D_MODEL),
            y_sample.reshape(DEC_BATCH, DEC_SEQ, D_MODEL),
            heads(qkv32[tail, kcols], (1, 1, keep)),
            heads(qkv32[tail, vcols], (1, 1, keep)),
            heads(qkv32[new, kcols], (1, DEC_BATCH, DEC_SEQ)),
            heads(qkv32[new, vcols], (1, DEC_BATCH, DEC_SEQ)),
            uva[new, A_WIDTH:].reshape(1, DEC_BATCH, DEC_SEQ, A_WIDTH))
```

```python
import functools

import numpy as np
import jax
import jax.numpy as jnp
from jax import lax
from jax.experimental import pallas as pl
from jax.experimental.pallas import tpu as pltpu

D_MODEL = 4096
SEQ = 8192
DEC_BATCH = 8
DEC_SEQ = 32
T_PROMPT = SEQ
T_SAMPLE = DEC_BATCH * DEC_SEQ
T_ALL = T_PROMPT + T_SAMPLE

CHUNK = 64
A_WIDTH = D_MODEL // 2
A_GROUPS = 4
A_GROUP_DIM = A_WIDTH // A_GROUPS
A_CHUNK = 128
B_WIDTH = D_MODEL - A_WIDTH
B_HEAD_DIM = 128
B_HEADS = B_WIDTH // B_HEAD_DIM
B_PAST_REACH = 8 * CHUNK
REL_MAX = 256
N_GROUPS = 8
EXPERTS_PER_GROUP = 8
N_EXPERTS = N_GROUPS * EXPERTS_PER_GROUP
D_EXPERT = D_MODEL // 4
EPS = 1e-6
NEG = -1e30

LANES = 128
VMEM_LIMIT_BYTES = 56 * 1024 * 1024
ROW_TILE = 256
COMBINE_TILE = 128
MM_TM = 768
MM_TN = 1024
ATT_Q = 256
ATT_KB = B_PAST_REACH // ATT_Q + 1
SGU_CHUNKS_PER_STEP = 4
MOE_CH = 384
MOE_SUB = 96
MOE_TF = 256
MOE_SLAB = 1024
MOE_NC = (2 * T_ALL) // MOE_CH + N_EXPERTS
ROUTER_COLS = 128
DMA_LOOP_UNROLL = 8

assert T_ALL % MM_TM == 0 and T_PROMPT % ROW_TILE == 0 and T_SAMPLE == ROW_TILE


def _cparams(n_axes):
    return pltpu.CompilerParams(
        dimension_semantics=("arbitrary",) * n_axes,
        vmem_limit_bytes=VMEM_LIMIT_BYTES)


def _rms_scale(x):
    return lax.rsqrt(jnp.mean(x * x, axis=-1, keepdims=True) + EPS)


N_PROMPT_TILES = T_PROMPT // ROW_TILE
N_ROW_TILES = T_ALL // ROW_TILE


def _prenorm_kernel(xp_ref, xs_ref, g_ref, o_ref):
    i = pl.program_id(0)

    def norm(x):
        return (x * _rms_scale(x) * g_ref[...]).astype(o_ref.dtype)

    @pl.when(i < N_PROMPT_TILES)
    def _():
        o_ref[...] = norm(xp_ref[...])

    @pl.when(i >= N_PROMPT_TILES)
    def _():
        o_ref[...] = norm(xs_ref[...])


def _prompt_tile(i):
    return (jnp.minimum(i, N_PROMPT_TILES - 1), 0)


def _sample_tile(i):
    return (jnp.maximum(i - N_PROMPT_TILES, 0), 0)


def _prenorm(xp, xs, g):
    return pl.pallas_call(
        _prenorm_kernel,
        out_shape=jax.ShapeDtypeStruct((T_ALL, D_MODEL), jnp.bfloat16),
        grid=(N_ROW_TILES,),
        in_specs=[pl.BlockSpec((ROW_TILE, D_MODEL), _prompt_tile),
                  pl.BlockSpec((ROW_TILE, D_MODEL), _sample_tile),
                  pl.BlockSpec((1, D_MODEL), lambda i: (0, 0))],
        out_specs=pl.BlockSpec((ROW_TILE, D_MODEL), lambda i: (i, 0)),
        compiler_params=_cparams(1),
        name="prenorm",
    )(xp, xs, g)


def _uva_kernel(x_ref, w_ref, vg_ref, o_ref):
    is_va = pl.program_id(0) >= A_WIDTH // MM_TN
    x = x_ref[...]
    for k in range(MM_TN // A_GROUP_DIM):
        cols = slice(k * A_GROUP_DIM, (k + 1) * A_GROUP_DIM)
        g = jax.nn.gelu(jnp.dot(x, w_ref[:, cols], preferred_element_type=jnp.float32))
        scale = jnp.where(is_va, _rms_scale(g) * vg_ref[:, cols], 1.0)
        o_ref[:, cols] = g * scale


def _qkv_kernel(x_ref, w_ref, o16_ref, o32_ref):
    acc = jnp.dot(x_ref[...], w_ref[...], preferred_element_type=jnp.float32)
    o16_ref[...] = acc.astype(o16_ref.dtype)
    o32_ref[...] = acc


def _plain_mm_kernel(x_ref, w_ref, o_ref):
    o_ref[...] = jnp.dot(x_ref[...], w_ref[...], preferred_element_type=jnp.float32)


def _proj_uva(xn, w_in16, v_gain):
    n_va_tiles = A_WIDTH // MM_TN
    return pl.pallas_call(
        _uva_kernel,
        out_shape=jax.ShapeDtypeStruct((T_ALL, 2 * A_WIDTH), jnp.float32),
        grid=(2 * A_WIDTH // MM_TN, T_ALL // MM_TM),
        in_specs=[pl.BlockSpec((MM_TM, D_MODEL), lambda j, i: (i, 0)),
                  pl.BlockSpec((D_MODEL, MM_TN), lambda j, i: (0, j)),
                  pl.BlockSpec((1, MM_TN), lambda j, i: (0, jnp.maximum(j - n_va_tiles, 0)))],
        out_specs=pl.BlockSpec((MM_TM, MM_TN), lambda j, i: (i, j)),
        compiler_params=_cparams(2),
        name="proj_uva",
    )(xn, w_in16, v_gain)


def _proj_qkv(xn, w_in16):
    col0 = 2 * A_WIDTH // MM_TN
    n_cols = 3 * B_WIDTH
    shape = (T_ALL, n_cols)
    return pl.pallas_call(
        _qkv_kernel,
        out_shape=(jax.ShapeDtypeStruct(shape, jnp.bfloat16),
                   jax.ShapeDtypeStruct(shape, jnp.float32)),
        grid=(n_cols // MM_TN, T_ALL // MM_TM),
        in_specs=[pl.BlockSpec((MM_TM, D_MODEL), lambda j, i: (i, 0)),
                  pl.BlockSpec((D_MODEL, MM_TN), lambda j, i: (0, col0 + j))],
        out_specs=(pl.BlockSpec((MM_TM, MM_TN), lambda j, i: (i, j)),
                   pl.BlockSpec((MM_TM, MM_TN), lambda j, i: (i, j))),
        compiler_params=_cparams(2),
        name="proj_qkv",
    )(xn, w_in16)


def _proj_out(mix, w_out16):
    return pl.pallas_call(
        _plain_mm_kernel,
        out_shape=jax.ShapeDtypeStruct((T_ALL, D_MODEL), jnp.float32),
        grid=(D_MODEL // MM_TN, T_ALL // MM_TM),
        in_specs=[pl.BlockSpec((MM_TM, D_MODEL), lambda j, i: (i, 0)),
                  pl.BlockSpec((D_MODEL, MM_TN), lambda j, i: (0, j))],
        out_specs=pl.BlockSpec((MM_TM, MM_TN), lambda j, i: (i, j)),
        compiler_params=_cparams(2),
        name="proj_out",
    )(mix, w_out16)


def _sgu_kernel(u_ref, v_ref, w_ref, b_ref, ga_ref, *rest, chunk_len, n_chunks):
    o_ref = rest[-1]
    L = chunk_len
    row = lax.broadcasted_iota(jnp.int32, (L, L), 0)
    col = lax.broadcasted_iota(jnp.int32, (L, L), 1)
    w16 = [jnp.where(col <= row, w_ref[g], 0.0).astype(jnp.bfloat16)
           for g in range(A_GROUPS)]
    for c in range(n_chunks):
        rows = slice(c * L, (c + 1) * L)
        parts = []
        ss = jnp.zeros((L, 1), jnp.float32)
        for g in range(A_GROUPS):
            cols = slice(g * A_GROUP_DIM, (g + 1) * A_GROUP_DIM)
            f = jnp.dot(w16[g], v_ref[rows, cols].astype(jnp.bfloat16),
                        preferred_element_type=jnp.float32) + b_ref[:, g:g + 1]
            a = u_ref[rows, cols] * f
            parts.append(a)
            ss = ss + jnp.sum(a * a, axis=-1, keepdims=True)
        r = lax.rsqrt(ss * (1.0 / A_WIDTH) + EPS)
        for g in range(A_GROUPS):
            cols = slice(g * A_GROUP_DIM, (g + 1) * A_GROUP_DIM)
            o_ref[rows, cols] = (parts[g] * r * ga_ref[:, cols]).astype(o_ref.dtype)


def _sgu(uva, w_s, b_s_t, g_a, mix, *, chunk_len, n_chunks, row0, n_steps):
    rows = chunk_len * n_chunks
    rb0 = row0 // rows
    kernel = functools.partial(_sgu_kernel, chunk_len=chunk_len, n_chunks=n_chunks)
    return pl.pallas_call(
        kernel,
        out_shape=jax.ShapeDtypeStruct((T_ALL, D_MODEL), jnp.bfloat16),
        grid=(n_steps,),
        in_specs=[pl.BlockSpec((rows, A_WIDTH), lambda i: (rb0 + i, 0)),
                  pl.BlockSpec((rows, A_WIDTH), lambda i: (rb0 + i, 1)),
                  pl.BlockSpec((A_GROUPS, chunk_len, chunk_len), lambda i: (0, 0, 0)),
                  pl.BlockSpec((chunk_len, A_GROUPS), lambda i: (0, 0)),
                  pl.BlockSpec((1, A_WIDTH), lambda i: (0, 0)),
                  pl.BlockSpec(memory_space=pl.ANY)],
        out_specs=pl.BlockSpec((rows, A_WIDTH), lambda i: (rb0 + i, 0)),
        input_output_aliases={5: 0},
        compiler_params=_cparams(1),
        name="sgu_%d" % chunk_len,
    )(uva, uva, w_s, b_s_t, g_a, mix)


_LOG2E = 1.4426950408889634
_SCALE2 = B_HEAD_DIM ** -0.5 * _LOG2E
_NT = (((1,), (1,)), ((), ()))


def _attn_prompt_kernel(q_ref, *rest):
    k_refs = rest[:ATT_KB]
    v_refs = rest[ATT_KB:2 * ATT_KB]
    bias_ref, gb_ref, _, o_ref, acc_ref = rest[2 * ATT_KB:]
    i = pl.program_id(0)
    for h in range(B_HEADS):
        hs = slice(h * B_HEAD_DIM, (h + 1) * B_HEAD_DIM)
        q = q_ref[:, hs]
        s = []
        for b in range(ATT_KB):
            sb = lax.dot_general(q, k_refs[b][:, hs], _NT,
                                 preferred_element_type=jnp.float32)
            sb = sb * _SCALE2 + bias_ref[h, :, b * ATT_Q:(b + 1) * ATT_Q]
            if b < ATT_KB - 1:
                sb = jnp.where(i + b >= ATT_KB - 1, sb, NEG)
            s.append(sb)
        m = functools.reduce(jnp.maximum,
                             [jnp.max(sb, axis=-1, keepdims=True) for sb in s])
        p = [jnp.exp2(sb - m) for sb in s]
        l = functools.reduce(lambda a, b: a + b,
                             [jnp.sum(pb, axis=-1, keepdims=True) for pb in p])
        o = functools.reduce(lambda a, b: a + b, [
            jnp.dot(p[b].astype(jnp.bfloat16), v_refs[b][:, hs],
                    preferred_element_type=jnp.float32) for b in range(ATT_KB)])
        acc_ref[:, hs] = o / l
    o = acc_ref[...]
    o_ref[...] = (o * _rms_scale(o) * gb_ref[...]).astype(o_ref.dtype)


def _attn_prompt(qkv16, bias, g_b, mix):
    n_blocks = T_PROMPT // ATT_Q

    def kv_spec(b, col):
        return pl.BlockSpec(
            (ATT_Q, B_WIDTH),
            lambda i: (jnp.maximum(i + b - (ATT_KB - 1), 0), col))

    in_specs = ([pl.BlockSpec((ATT_Q, B_WIDTH), lambda i: (i, 0))]
                + [kv_spec(b, 1) for b in range(ATT_KB)]
                + [kv_spec(b, 2) for b in range(ATT_KB)]
                + [pl.BlockSpec((B_HEADS, ATT_Q, ATT_KB * ATT_Q), lambda i: (0, 0, 0)),
                   pl.BlockSpec((1, B_WIDTH), lambda i: (0, 0)),
                   pl.BlockSpec(memory_space=pl.ANY)])
    n_in = len(in_specs)
    return pl.pallas_call(
        _attn_prompt_kernel,
        out_shape=jax.ShapeDtypeStruct((T_ALL, D_MODEL), jnp.bfloat16),
        grid=(n_blocks,),
        in_specs=in_specs,
        out_specs=pl.BlockSpec((ATT_Q, B_WIDTH), lambda i: (i, 1)),
        scratch_shapes=[pltpu.VMEM((ATT_Q, B_WIDTH), jnp.float32)],
        input_output_aliases={n_in - 1: 0},
        compiler_params=_cparams(1),
        name="attn_prompt",
    )(qkv16, *([qkv16] * (2 * ATT_KB)), bias, g_b, mix)


def _attn_sample_kernel(q_ref, kn_ref, vn_ref, kc_ref, vc_ref, bias_ref, gb_ref, _,
                        o_ref, acc_ref):
    n_cache = kc_ref.shape[0] // B_HEADS
    for h in range(B_HEADS):
        hs = slice(h * B_HEAD_DIM, (h + 1) * B_HEAD_DIM)
        q = q_ref[:, hs]
        kc = kc_ref[pl.ds(h, n_cache, stride=B_HEADS), :].astype(jnp.bfloat16)
        vc = vc_ref[pl.ds(h, n_cache, stride=B_HEADS), :].astype(jnp.bfloat16)
        sc = lax.dot_general(q, kc, _NT, preferred_element_type=jnp.float32)
        sc = sc * _SCALE2 + bias_ref[h, :, :n_cache]
        sn = lax.dot_general(q, kn_ref[:, hs], _NT, preferred_element_type=jnp.float32)
        sn = sn * _SCALE2 + bias_ref[h, :, n_cache:n_cache + DEC_SEQ]
        m = jnp.maximum(jnp.max(sc, axis=-1, keepdims=True),
                        jnp.max(sn, axis=-1, keepdims=True))
        pc = jnp.exp2(sc - m)
        pn = jnp.exp2(sn - m)
        l = jnp.sum(pc, axis=-1, keepdims=True) + jnp.sum(pn, axis=-1, keepdims=True)
        o = (jnp.dot(pc.astype(jnp.bfloat16), vc, preferred_element_type=jnp.float32)
             + jnp.dot(pn.astype(jnp.bfloat16), vn_ref[:, hs],
                       preferred_element_type=jnp.float32))
        acc_ref[:, hs] = o / l
    o = acc_ref[...]
    o_ref[...] = (o * _rms_scale(o) * gb_ref[...]).astype(o_ref.dtype)


def _attn_sample(qkv16, cache_k, cache_v, bias, g_b, mix):
    n_cache = cache_k.shape[1] // B_HEADS
    rb0 = T_PROMPT // DEC_SEQ
    new_spec = lambda col: pl.BlockSpec((DEC_SEQ, B_WIDTH), lambda b: (rb0 + b, col))
    cache_spec = pl.BlockSpec((None, n_cache * B_HEADS, B_HEAD_DIM), lambda b: (b, 0, 0))
    return pl.pallas_call(
        _attn_sample_kernel,
        out_shape=jax.ShapeDtypeStruct((T_ALL, D_MODEL), jnp.bfloat16),
        grid=(DEC_BATCH,),
        in_specs=[new_spec(0), new_spec(1), new_spec(2), cache_spec, cache_spec,
                  pl.BlockSpec(bias.shape, lambda b: (0, 0, 0)),
                  pl.BlockSpec((1, B_WIDTH), lambda b: (0, 0)),
                  pl.BlockSpec(memory_space=pl.ANY)],
        out_specs=pl.BlockSpec((DEC_SEQ, B_WIDTH), lambda b: (rb0 + b, 1)),
        scratch_shapes=[pltpu.VMEM((DEC_SEQ, B_WIDTH), jnp.float32)],
        input_output_aliases={7: 0},
        compiler_params=_cparams(1),
        name="attn_sample",
    )(qkv16, qkv16, qkv16, cache_k, cache_v, bias, g_b, mix)


def _rel_bias_log2(table, n_q, n_k, offset):
    width = -(-(n_q + n_k - 2) // LANES) * LANES
    period = width + 1
    t = np.arange(period)
    t = np.where(t < n_k, t, t - period)
    t = np.clip(t, -(n_q - 1), n_k - 1)
    v = table[:, np.clip(offset - t, -(CHUNK - 1), REL_MAX) + (CHUNK - 1)].astype(jnp.float32)
    v = v * _LOG2E
    return jnp.tile(v, (1, n_q))[:, :n_q * width].reshape(-1, n_q, width)


def _prompt_bias(table):
    bias = _rel_bias_log2(table, ATT_Q, ATT_KB * ATT_Q, B_PAST_REACH)
    qc = np.arange(ATT_Q)[:, None] // CHUNK
    kc = np.arange(bias.shape[2])[None, :] // CHUNK
    band = (kc >= qc) & (kc <= qc + B_PAST_REACH // CHUNK)
    return jnp.where(band[None], bias, NEG)


def _split_bf16(x):
    hi = x.astype(jnp.bfloat16)
    lo = (x - hi.astype(jnp.float32)).astype(jnp.bfloat16)
    return hi, lo


def _router_kernel(xp_ref, xs_ref, hp_ref, g_ref, wr_ref, br_ref,
                   h_ref, xt_ref, eid_ref, gate_ref):
    i = pl.program_id(0)

    @pl.when(i < N_PROMPT_TILES)
    def _():
        h_ref[...] = xp_ref[...] + hp_ref[...]

    @pl.when(i >= N_PROMPT_TILES)
    def _():
        h_ref[...] = xs_ref[...] + hp_ref[...]

    h = h_ref[...]
    xt = h * _rms_scale(h) * g_ref[...]
    kd = D_MODEL // LANES
    for k in range(_pitch(kd)):
        piece = (xt[:, k * LANES:(k + 1) * LANES] if k < kd
                 else jnp.zeros((ROW_TILE, LANES), jnp.float32))
        xt_ref[pl.ds(k, ROW_TILE, stride=_pitch(kd)), :] = piece

    xh, xl = _split_bf16(xt)
    wh, wl = _split_bf16(wr_ref[...])
    dot = functools.partial(jnp.dot, preferred_element_type=jnp.float32)
    logits = dot(xh, wh) + (dot(xl, wh) + dot(xh, wl)) + br_ref[...]

    lane = lax.broadcasted_iota(jnp.int32, logits.shape, 1)
    big = jnp.int32(ROUTER_COLS)

    def masked_max(x, mask):
        return jnp.max(jnp.where(mask, x, NEG), axis=-1, keepdims=True)

    def first_lane(mask):
        return jnp.min(jnp.where(mask, lane, big), axis=-1, keepdims=True)

    gmask = lane < N_GROUPS
    gmax = masked_max(logits, gmask)
    gsum = jnp.sum(jnp.where(gmask, jnp.exp(logits - gmax), 0.0), axis=-1, keepdims=True)
    g_sel = first_lane(gmask & (logits == gmax))
    pg_top = 1.0 / gsum

    e_lo = N_GROUPS + g_sel * EXPERTS_PER_GROUP
    emask = (lane >= e_lo) & (lane < e_lo + EXPERTS_PER_GROUP)
    emax = masked_max(logits, emask)
    ex = jnp.where(emask, jnp.exp(logits - emax), 0.0)
    pe = ex / jnp.sum(ex, axis=-1, keepdims=True)
    p1 = jnp.max(pe, axis=-1, keepdims=True)
    l1 = first_lane(emask & (pe == p1))
    rest = emask & (lane != l1)
    p2 = jnp.max(jnp.where(rest, pe, -1.0), axis=-1, keepdims=True)
    l2 = first_lane(rest & (pe == p2))
    denom = p1 + p2
    g1 = pg_top * p1 / denom
    g2 = pg_top * p2 / denom
    eid_ref[...] = jnp.where(lane == 0, l1 - N_GROUPS,
                             jnp.where(lane == 1, l2 - N_GROUPS, 0))
    gate_ref[...] = jnp.where(lane == 0, g1, jnp.where(lane == 1, g2, 0.0))


def _router(xp, xs, hproj, g_ffn, w_router, b_router):
    tile = lambda cols: pl.BlockSpec((ROW_TILE, cols), lambda i: (i, 0))
    return pl.pallas_call(
        _router_kernel,
        out_shape=(jax.ShapeDtypeStruct((T_ALL, D_MODEL), jnp.float32),
                   jax.ShapeDtypeStruct((T_ALL * _pitch(D_MODEL // LANES), LANES), jnp.float32),
                   jax.ShapeDtypeStruct((T_ALL, ROUTER_COLS), jnp.int32),
                   jax.ShapeDtypeStruct((T_ALL, ROUTER_COLS), jnp.float32)),
        grid=(N_ROW_TILES,),
        in_specs=[pl.BlockSpec((ROW_TILE, D_MODEL), _prompt_tile),
                  pl.BlockSpec((ROW_TILE, D_MODEL), _sample_tile),
                  tile(D_MODEL),
                  pl.BlockSpec((1, D_MODEL), lambda i: (0, 0)),
                  pl.BlockSpec((D_MODEL, ROUTER_COLS), lambda i: (0, 0)),
                  pl.BlockSpec((1, ROUTER_COLS), lambda i: (0, 0))],
        out_specs=(tile(D_MODEL),
                   pl.BlockSpec((ROW_TILE * _pitch(D_MODEL // LANES), LANES), lambda i: (i, 0)),
                   tile(ROUTER_COLS), tile(ROUTER_COLS)),
        compiler_params=_cparams(1),
        name="router",
    )(xp, xs, hproj, g_ffn, w_router, b_router)


def _pitch(k):
    return k + 4 if k % 8 == 0 else k


def _row_copy(src_hbm, src_row, buf, slot, dst_row, sem, k):
    p = _pitch(k)
    return pltpu.make_async_copy(src_hbm.at[pl.ds(src_row * p, k)],
                                 buf.at[slot, pl.ds(dst_row * p, k)], sem.at[slot])


def _gather_rows_start(src_hbm, row_of, row0, n_rows, buf, slot, sem, k):
    def body(r, carry):
        _row_copy(src_hbm, row_of(row0 + r), buf, slot, row0 + r, sem, k).start()
        return carry
    lax.fori_loop(0, n_rows, body, 0, unroll=DMA_LOOP_UNROLL)


def _gather_rows_wait(src_hbm, row0, n_rows, buf, slot, sem, k):
    def body(r, carry):
        _row_copy(src_hbm, 0, buf, slot, row0 + r, sem, k).wait()
        return carry
    lax.fori_loop(0, n_rows, body, 0, unroll=DMA_LOOP_UNROLL)


def _rows_2d(buf, slot, row0, m, k):
    p = _pitch(k)
    return jnp.concatenate(
        [buf[slot, pl.ds(row0 * p + kk, m, stride=p), :] for kk in range(k)], axis=-1)


def _row_groups(n_rows, ch, sub, fn):
    for g in range(ch // sub):
        @pl.when(g * sub < n_rows)
        def _():
            fn(g * sub)


def _sort_rows_kernel(cm_ref, cb_ref, nv_ref, tok_ref, xt_hbm, o_ref, xbuf, sem, *, ch, sub):
    c = pl.program_id(0)
    nv = nv_ref[0]
    kd = o_ref.shape[1] // LANES
    last_tok = tok_ref.shape[0] - 1

    def start_gather(cc):
        base = cb_ref[cc]
        _row_groups(cm_ref[cc], ch, sub, lambda r0: _gather_rows_start(
            xt_hbm, lambda r: tok_ref[jnp.minimum(base + r, last_tok)], r0, sub,
            xbuf, cc % 2, sem, kd))

    @pl.when(c == 0)
    def _():
        start_gather(0)

    @pl.when(c + 1 < nv)
    def _():
        start_gather(c + 1)

    @pl.when(c < nv)
    def _():
        slot = c % 2
        _row_groups(cm_ref[c], ch, sub,
                    lambda r0: _gather_rows_wait(xt_hbm, r0, sub, xbuf, slot, sem, kd))
        for m in range(sub, ch + 1, sub):
            @pl.when(cm_ref[c] == m)
            def _():
                o_ref[:m, :] = _rows_2d(xbuf, slot, 0, m, kd).astype(o_ref.dtype)
                if m < ch:
                    o_ref[m:, :] = jnp.zeros((ch - m, o_ref.shape[1]), o_ref.dtype)


def _sort_rows(plan, xt_rows, d_model, *, ch=MOE_CH, sub=MOE_SUB):
    chunk_expert, chunk_mrows, chunk_base, n_chunks, sorted_tok = plan
    nc = chunk_expert.shape[0]
    pd = _pitch(d_model // LANES)
    grid_spec = pltpu.PrefetchScalarGridSpec(
        num_scalar_prefetch=4,
        grid=(nc,),
        in_specs=[pl.BlockSpec(memory_space=pl.ANY)],
        out_specs=pl.BlockSpec((ch, d_model),
                               lambda c, cm, cb, nv, tok: (jnp.minimum(c, nv[0] - 1), 0)),
        scratch_shapes=[pltpu.VMEM((2, ch * pd, LANES), jnp.float32),
                        pltpu.SemaphoreType.DMA((2,))],
    )
    return pl.pallas_call(
        functools.partial(_sort_rows_kernel, ch=ch, sub=sub),
        out_shape=jax.ShapeDtypeStruct((nc * ch, d_model), jnp.bfloat16),
        grid_spec=grid_spec,
        compiler_params=_cparams(1),
        name="sort_rows",
    )(chunk_mrows, chunk_base, n_chunks, sorted_tok, xt_rows)


def _expert_kernel(ce_ref, cm_ref, nv_ref, x_ref, wg_ref, wu_ref, wd_ref, o_ref,
                   *, ch, sub, slab):
    c = pl.program_id(0)
    j = pl.program_id(1)
    d_model, tf = wg_ref.shape
    kd = d_model // LANES
    pd = _pitch(kd)

    def dot16(a, w):
        return jnp.dot(a, w.astype(jnp.bfloat16), preferred_element_type=jnp.float32)

    @pl.when((c < nv_ref[0]) & (j == 0))
    def _():
        o_ref[...] = jnp.zeros(o_ref.shape, o_ref.dtype)

    @pl.when(c < nv_ref[0])
    def _():
        for m in range(sub, ch + 1, sub):
            @pl.when(cm_ref[c] == m)
            def _():
                x = x_ref[:m, :]
                hg = jnp.zeros((m, tf), jnp.float32)
                hu = jnp.zeros((m, tf), jnp.float32)
                for s in range(d_model // slab):
                    ks = slice(s * slab, (s + 1) * slab)
                    hg = hg + dot16(x[:, ks], wg_ref[ks, :])
                    hu = hu + dot16(x[:, ks], wu_ref[ks, :])
                hh = (jax.nn.silu(hg) * hu).astype(jnp.bfloat16)
                for s in range(d_model // slab):
                    y = dot16(hh, wd_ref[:, s * slab:(s + 1) * slab])
                    for kk in range(slab // LANES):
                        k = s * (slab // LANES) + kk
                        o_ref[pl.ds(k, m, stride=pd), :] += y[:, kk * LANES:(kk + 1) * LANES]


def _experts(plan, x_sorted, w_gate, w_up, w_down, *,
             ch=MOE_CH, sub=MOE_SUB, tf=MOE_TF, slab=MOE_SLAB):
    chunk_expert, chunk_mrows, _, n_chunks, _ = plan
    nc = chunk_expert.shape[0]
    d_model, d_expert = w_gate.shape[1], w_gate.shape[2]
    nf = d_expert // tf
    pd = _pitch(d_model // LANES)

    def chunk_of(c, nv):
        return jnp.minimum(c, nv[0] - 1)

    def ftile_of(c, j, nv):
        return jnp.where(c < nv[0], j, nf - 1)

    grid_spec = pltpu.PrefetchScalarGridSpec(
        num_scalar_prefetch=3,
        grid=(nc, nf),
        in_specs=[
            pl.BlockSpec((ch, d_model), lambda c, j, ce, cm, nv: (chunk_of(c, nv), 0)),
            pl.BlockSpec((None, d_model, tf),
                         lambda c, j, ce, cm, nv: (ce[chunk_of(c, nv)], 0, ftile_of(c, j, nv))),
            pl.BlockSpec((None, d_model, tf),
                         lambda c, j, ce, cm, nv: (ce[chunk_of(c, nv)], 0, ftile_of(c, j, nv))),
            pl.BlockSpec((None, tf, d_model),
                         lambda c, j, ce, cm, nv: (ce[chunk_of(c, nv)], ftile_of(c, j, nv), 0)),
        ],
        out_specs=pl.BlockSpec((ch * pd, LANES), lambda c, j, ce, cm, nv: (chunk_of(c, nv), 0)),
    )
    return pl.pallas_call(
        functools.partial(_expert_kernel, ch=ch, sub=sub, slab=slab),
        out_shape=jax.ShapeDtypeStruct((nc * ch * pd, LANES), jnp.float32),
        grid_spec=grid_spec,
        compiler_params=_cparams(2),
        name="experts",
    )(chunk_expert, chunk_mrows, n_chunks, x_sorted, w_gate, w_up, w_down)


def _moe_plan(eid, *, ch=MOE_CH, sub=MOE_SUB, nc=MOE_NC, n_experts=N_EXPERTS):
    i32 = jnp.int32
    flat_e = eid.reshape(-1)
    order = jnp.argsort(flat_e, stable=True).astype(i32)
    rank = jnp.argsort(order).astype(i32)
    onehot_e = flat_e[:, None] == jnp.arange(n_experts, dtype=i32)[None, :]
    counts = jnp.sum(onehot_e, axis=0, dtype=i32)
    start = jnp.cumsum(counts) - counts
    nchunk = (counts + ch - 1) // ch
    cend = jnp.cumsum(nchunk)
    cstart = cend - nchunk
    pos = rank + jnp.sum(jnp.where(onehot_e, (cstart * ch - start)[None, :], 0), axis=1)
    n_chunks = cend[-1:]
    cidx = jnp.arange(nc, dtype=i32)
    chunk_expert = jnp.minimum(jnp.sum(cend[None, :] <= cidx[:, None], axis=1, dtype=i32),
                               n_experts - 1)
    onehot_c = chunk_expert[:, None] == jnp.arange(n_experts, dtype=i32)[None, :]
    lookup = lambda table: jnp.sum(jnp.where(onehot_c, table[None, :], 0), axis=1)
    offset = (cidx - lookup(cstart)) * ch
    chunk_rows = jnp.clip(lookup(counts) - offset, 0, ch)
    chunk_rows = jnp.where(cidx < n_chunks[0], chunk_rows, 0)
    chunk_mrows = (chunk_rows + sub - 1) // sub * sub
    chunk_base = lookup(start) + offset
    return (chunk_expert, chunk_mrows.astype(i32), chunk_base.astype(i32),
            n_chunks.astype(i32), order // 2), pos


def _combine_kernel(pos_ref, posn_ref, h_ref, gate_ref, g_ref, y_hbm, op_ref, os_ref,
                    ybuf, sem, *, n_prompt_tiles):
    i = pl.program_id(0)
    n = pl.num_programs(0)
    tile = h_ref.shape[0]
    n_rows = 2 * tile
    kd = h_ref.shape[1] // LANES

    @pl.when(i == 0)
    def _():
        _gather_rows_start(y_hbm, lambda r: pos_ref[0, 0, r], 0, n_rows, ybuf, 0, sem, kd)

    @pl.when(i + 1 < n)
    def _():
        _gather_rows_start(y_hbm, lambda r: posn_ref[0, 0, r], 0, n_rows, ybuf, (i + 1) % 2,
                           sem, kd)

    _gather_rows_wait(y_hbm, 0, n_rows, ybuf, i % 2, sem, kd)
    slot = i % 2
    x = (h_ref[...] + gate_ref[:, 0:1] * _rows_2d(ybuf, slot, 0, tile, kd)
         + gate_ref[:, 1:2] * _rows_2d(ybuf, slot, tile, tile, kd))
    y = x * _rms_scale(x) * g_ref[...]

    @pl.when(i < n_prompt_tiles)
    def _():
        op_ref[...] = y

    @pl.when(i >= n_prompt_tiles)
    def _():
        os_ref[...] = y


def _combine(h, y_rows, pos, gates, g_final, *, t_prompt=T_PROMPT, tile=COMBINE_TILE):
    t_all, d_model = h.shape
    n_tiles = t_all // tile
    n_prompt_tiles = t_prompt // tile
    idx = pos.reshape(n_tiles, tile, 2).transpose(0, 2, 1).reshape(n_tiles, 1, 2 * tile)
    smem = lambda f: pl.BlockSpec((1, 1, 2 * tile), f, memory_space=pltpu.SMEM)
    return pl.pallas_call(
        functools.partial(_combine_kernel, n_prompt_tiles=n_prompt_tiles),
        out_shape=(jax.ShapeDtypeStruct((t_prompt, d_model), jnp.float32),
                   jax.ShapeDtypeStruct((t_all - t_prompt, d_model), jnp.float32)),
        grid=(n_tiles,),
        in_specs=[smem(lambda i: (i, 0, 0)),
                  smem(lambda i: (jnp.minimum(i + 1, n_tiles - 1), 0, 0)),
                  pl.BlockSpec((tile, d_model), lambda i: (i, 0)),
                  pl.BlockSpec((tile, gates.shape[1]), lambda i: (i, 0)),
                  pl.BlockSpec((1, d_model), lambda i: (0, 0)),
                  pl.BlockSpec(memory_space=pl.ANY)],
        out_specs=(pl.BlockSpec((tile, d_model),
                                lambda i: (jnp.minimum(i, n_prompt_tiles - 1), 0)),
                   pl.BlockSpec((tile, d_model),
                                lambda i: (jnp.maximum(i - n_prompt_tiles, 0), 0))),
        scratch_shapes=[pltpu.VMEM((2, 2 * tile * _pitch(d_model // LANES), LANES),
                                   jnp.float32),
                        pltpu.SemaphoreType.DMA((2,))],
        compiler_params=_cparams(1),
        name="combine",
    )(idx, idx, h, gates, g_final, y_rows)


def kernel(x_prompt, x_sample, cache_b_k, cache_b_v, g_mix, w_in, sgu_v_gain, sgu_w, sgu_b,
           rel_bias, g_out_a, g_out_b, w_out, g_ffn, w_router_group, b_router_group,
           w_router_expert, b_router_expert, w_gate, w_up, w_down, g_final):
    assert w_in.shape[0] == 1, "one layer"
    xp = x_prompt.reshape(T_PROMPT, D_MODEL)
    xs = x_sample.reshape(T_SAMPLE, D_MODEL)
    row = lambda v: v.reshape(1, -1)

    xn = _prenorm(xp, xs, row(g_mix[0]))
    w_in16 = w_in[0].astype(jnp.bfloat16)
    uva = _proj_uva(xn, w_in16, row(sgu_v_gain[0]))
    qkv16, qkv32 = _proj_qkv(xn, w_in16)

    w_s, b_s = sgu_w[0], sgu_b[0]
    mix = _sgu(uva, w_s, b_s.T, row(g_out_a[0]), xn,
               chunk_len=A_CHUNK, n_chunks=SGU_CHUNKS_PER_STEP, row0=0,
               n_steps=T_PROMPT // (A_CHUNK * SGU_CHUNKS_PER_STEP))
    mix = _sgu(uva, w_s[:, :DEC_SEQ, :DEC_SEQ], b_s[:, :DEC_SEQ].T, row(g_out_a[0]), mix,
               chunk_len=DEC_SEQ, n_chunks=DEC_BATCH, row0=T_PROMPT, n_steps=1)

    table = rel_bias[0]
    n_cache = cache_b_k.shape[2]
    mix = _attn_prompt(qkv16, _prompt_bias(table), row(g_out_b[0]), mix)
    bias_s = _rel_bias_log2(table, DEC_SEQ, n_cache + DEC_SEQ, n_cache)
    mix = _attn_sample(qkv16, cache_b_k[0].reshape(DEC_BATCH, n_cache * B_HEADS, B_HEAD_DIM),
                       cache_b_v[0].reshape(DEC_BATCH, n_cache * B_HEADS, B_HEAD_DIM),
                       bias_s, row(g_out_b[0]), mix)

    hproj = _proj_out(mix, w_out[0].astype(jnp.bfloat16))

    pad = ROUTER_COLS - N_GROUPS - N_EXPERTS
    w_router = jnp.pad(jnp.concatenate([w_router_group[0], w_router_expert[0]], axis=1),
                       ((0, 0), (0, pad)))
    b_router = jnp.pad(jnp.concatenate([b_router_group[0], b_router_expert[0]]), (0, pad))
    h, xt_rows, eid, gates = _router(xp, xs, hproj, row(g_ffn[0]), w_router, row(b_router))

    plan, pos = _moe_plan(eid[:, :2])
    x_sorted = _sort_rows(plan, xt_rows, D_MODEL)
    y_rows = _experts(plan, x_sorted, w_gate[0], w_up[0], w_down[0])
    y_prompt, y_sample = _combine(h, y_rows, pos, gates, row(g_final))

    keep = min(B_PAST_REACH, SEQ)
    kcols = slice(B_WIDTH, 2 * B_WIDTH)
    vcols = slice(2 * B_WIDTH, 3 * B_WIDTH)
    tail = slice(T_PROMPT - keep, T_PROMPT)
    new = slice(T_PROMPT, T_ALL)
    heads = lambda t, lead: t.reshape(*lead, B_HEADS, B_HEAD_DIM)
    return (y_prompt.reshape(1, SEQ, D_MODEL),
            y_sample.reshape(DEC_BATCH, DEC_SEQ, D_MODEL),
            heads(qkv32[tail, kcols], (1, 1, keep)),
            heads(qkv32[tail, vcols], (1, 1, keep)),
            heads(qkv32[new, kcols], (1, DEC_BATCH, DEC_SEQ)),
            heads(qkv32[new, vcols], (1, DEC_BATCH, DEC_SEQ)),
            uva[new, A_WIDTH:].reshape(1, DEC_BATCH, DEC_SEQ, A_WIDTH))
```

```python
import functools

import numpy as np
import jax
import jax.numpy as jnp
from jax import lax
from jax.experimental import pallas as pl
from jax.experimental.pallas import tpu as pltpu

D_MODEL = 4096
SEQ = 8192
DEC_BATCH = 8
DEC_SEQ = 32
T_PROMPT = SEQ
T_SAMPLE = DEC_BATCH * DEC_SEQ
T_ALL = T_PROMPT + T_SAMPLE

CHUNK = 64
A_WIDTH = D_MODEL // 2
A_GROUPS = 4
A_GROUP_DIM = A_WIDTH // A_GROUPS
A_CHUNK = 128
B_WIDTH = D_MODEL - A_WIDTH
B_HEAD_DIM = 128
B_HEADS = B_WIDTH // B_HEAD_DIM
B_PAST_REACH = 8 * CHUNK
REL_MAX = 256
N_GROUPS = 8
EXPERTS_PER_GROUP = 8
N_EXPERTS = N_GROUPS * EXPERTS_PER_GROUP
D_EXPERT = D_MODEL // 4
EPS = 1e-6
NEG = -1e30

LANES = 128
VMEM_LIMIT_BYTES = 56 * 1024 * 1024
ROW_TILE = 256
COMBINE_TILE = 128
MM_TM = 768
MM_TN = 1024
ATT_Q = 256
ATT_KB = B_PAST_REACH // ATT_Q + 1
SGU_CHUNKS_PER_STEP = 4
MOE_CH = 384
MOE_SUB = 96
MOE_TF = 256
MOE_KSLAB = 256
MOE_SLAB = 1024
MOE_NC = (2 * T_ALL) // MOE_CH + N_EXPERTS
ROUTER_COLS = 128
DMA_LOOP_UNROLL = 8

assert T_ALL % MM_TM == 0 and T_PROMPT % ROW_TILE == 0 and T_SAMPLE == ROW_TILE


def _cparams(n_axes):
    return pltpu.CompilerParams(
        dimension_semantics=("arbitrary",) * n_axes,
        vmem_limit_bytes=VMEM_LIMIT_BYTES)


def _rms_scale(x):
    return lax.rsqrt(jnp.mean(x * x, axis=-1, keepdims=True) + EPS)


N_PROMPT_TILES = T_PROMPT // ROW_TILE
N_ROW_TILES = T_ALL // ROW_TILE


def _prenorm_kernel(xp_ref, xs_ref, g_ref, o_ref):
    i = pl.program_id(0)

    def norm(x):
        return (x * _rms_scale(x) * g_ref[...]).astype(o_ref.dtype)

    @pl.when(i < N_PROMPT_TILES)
    def _():
        o_ref[...] = norm(xp_ref[...])

    @pl.when(i >= N_PROMPT_TILES)
    def _():
        o_ref[...] = norm(xs_ref[...])


def _prompt_tile(i):
    return (jnp.minimum(i, N_PROMPT_TILES - 1), 0)


def _sample_tile(i):
    return (jnp.maximum(i - N_PROMPT_TILES, 0), 0)


def _prenorm(xp, xs, g):
    return pl.pallas_call(
        _prenorm_kernel,
        out_shape=jax.ShapeDtypeStruct((T_ALL, D_MODEL), jnp.bfloat16),
        grid=(N_ROW_TILES,),
        in_specs=[pl.BlockSpec((ROW_TILE, D_MODEL), _prompt_tile),
                  pl.BlockSpec((ROW_TILE, D_MODEL), _sample_tile),
                  pl.BlockSpec((1, D_MODEL), lambda i: (0, 0))],
        out_specs=pl.BlockSpec((ROW_TILE, D_MODEL), lambda i: (i, 0)),
        compiler_params=_cparams(1),
        name="prenorm",
    )(xp, xs, g)


def _uva_kernel(x_ref, w_ref, vg_ref, o_ref):
    j = pl.program_id(0)
    acc = jnp.dot(x_ref[...], w_ref[...], preferred_element_type=jnp.float32)
    g = jax.nn.gelu(acc)

    @pl.when(j < A_WIDTH // MM_TN)
    def _():
        o_ref[...] = g

    @pl.when(j >= A_WIDTH // MM_TN)
    def _():
        for k in range(MM_TN // A_GROUP_DIM):
            cols = slice(k * A_GROUP_DIM, (k + 1) * A_GROUP_DIM)
            gg = g[:, cols]
            o_ref[:, cols] = gg * _rms_scale(gg) * vg_ref[:, cols]


def _qkv_kernel(x_ref, w_ref, o16_ref, o32_ref):
    acc = jnp.dot(x_ref[...], w_ref[...], preferred_element_type=jnp.float32)
    is_q = pl.program_id(0) < B_WIDTH // MM_TN
    o16_ref[...] = (acc * jnp.where(is_q, _SCALE2, 1.0)).astype(o16_ref.dtype)
    o32_ref[...] = acc


def _plain_mm_kernel(x_ref, w_ref, o_ref):
    o_ref[...] = jnp.dot(x_ref[...], w_ref[...], preferred_element_type=jnp.float32)


def _proj_uva(xn, w_in16, v_gain):
    n_va_tiles = A_WIDTH // MM_TN
    return pl.pallas_call(
        _uva_kernel,
        out_shape=jax.ShapeDtypeStruct((T_ALL, 2 * A_WIDTH), jnp.float32),
        grid=(2 * A_WIDTH // MM_TN, T_ALL // MM_TM),
        in_specs=[pl.BlockSpec((MM_TM, D_MODEL), lambda j, i: (i, 0)),
                  pl.BlockSpec((D_MODEL, MM_TN), lambda j, i: (0, j)),
                  pl.BlockSpec((1, MM_TN), lambda j, i: (0, jnp.maximum(j - n_va_tiles, 0)))],
        out_specs=pl.BlockSpec((MM_TM, MM_TN), lambda j, i: (i, j)),
        compiler_params=_cparams(2),
        name="proj_uva",
    )(xn, w_in16, v_gain)


def _proj_qkv(xn, w_in16):
    col0 = 2 * A_WIDTH // MM_TN
    n_cols = 3 * B_WIDTH
    shape = (T_ALL, n_cols)
    return pl.pallas_call(
        _qkv_kernel,
        out_shape=(jax.ShapeDtypeStruct(shape, jnp.bfloat16),
                   jax.ShapeDtypeStruct(shape, jnp.float32)),
        grid=(n_cols // MM_TN, T_ALL // MM_TM),
        in_specs=[pl.BlockSpec((MM_TM, D_MODEL), lambda j, i: (i, 0)),
                  pl.BlockSpec((D_MODEL, MM_TN), lambda j, i: (0, col0 + j))],
        out_specs=(pl.BlockSpec((MM_TM, MM_TN), lambda j, i: (i, j)),
                   pl.BlockSpec((MM_TM, MM_TN), lambda j, i: (i, j))),
        compiler_params=_cparams(2),
        name="proj_qkv",
    )(xn, w_in16)


def _proj_out(mix, w_out16):
    return pl.pallas_call(
        _plain_mm_kernel,
        out_shape=jax.ShapeDtypeStruct((T_ALL, D_MODEL), jnp.float32),
        grid=(D_MODEL // MM_TN, T_ALL // MM_TM),
        in_specs=[pl.BlockSpec((MM_TM, D_MODEL), lambda j, i: (i, 0)),
                  pl.BlockSpec((D_MODEL, MM_TN), lambda j, i: (0, j))],
        out_specs=pl.BlockSpec((MM_TM, MM_TN), lambda j, i: (i, j)),
        compiler_params=_cparams(2),
        name="proj_out",
    )(mix, w_out16)


def _sgu_kernel(u_ref, v_ref, w_ref, b_ref, ga_ref, *rest, chunk_len, n_chunks):
    o_ref = rest[-1]
    L = chunk_len
    row = lax.broadcasted_iota(jnp.int32, (L, L), 0)
    col = lax.broadcasted_iota(jnp.int32, (L, L), 1)
    w16 = [jnp.where(col <= row, w_ref[g], 0.0).astype(jnp.bfloat16)
           for g in range(A_GROUPS)]
    for c in range(n_chunks):
        rows = slice(c * L, (c + 1) * L)
        parts = []
        ss = jnp.zeros((L, 1), jnp.float32)
        for g in range(A_GROUPS):
            cols = slice(g * A_GROUP_DIM, (g + 1) * A_GROUP_DIM)
            f = jnp.dot(w16[g], v_ref[rows, cols].astype(jnp.bfloat16),
                        preferred_element_type=jnp.float32) + b_ref[:, g:g + 1]
            a = u_ref[rows, cols] * f
            parts.append(a)
            ss = ss + jnp.sum(a * a, axis=-1, keepdims=True)
        r = lax.rsqrt(ss * (1.0 / A_WIDTH) + EPS)
        for g in range(A_GROUPS):
            cols = slice(g * A_GROUP_DIM, (g + 1) * A_GROUP_DIM)
            o_ref[rows, cols] = (parts[g] * r * ga_ref[:, cols]).astype(o_ref.dtype)


def _sgu(uva, w_s, b_s_t, g_a, mix, *, chunk_len, n_chunks, row0, n_steps):
    rows = chunk_len * n_chunks
    rb0 = row0 // rows
    kernel = functools.partial(_sgu_kernel, chunk_len=chunk_len, n_chunks=n_chunks)
    return pl.pallas_call(
        kernel,
        out_shape=jax.ShapeDtypeStruct((T_ALL, D_MODEL), jnp.bfloat16),
        grid=(n_steps,),
        in_specs=[pl.BlockSpec((rows, A_WIDTH), lambda i: (rb0 + i, 0)),
                  pl.BlockSpec((rows, A_WIDTH), lambda i: (rb0 + i, 1)),
                  pl.BlockSpec((A_GROUPS, chunk_len, chunk_len), lambda i: (0, 0, 0)),
                  pl.BlockSpec((chunk_len, A_GROUPS), lambda i: (0, 0)),
                  pl.BlockSpec((1, A_WIDTH), lambda i: (0, 0)),
                  pl.BlockSpec(memory_space=pl.ANY)],
        out_specs=pl.BlockSpec((rows, A_WIDTH), lambda i: (rb0 + i, 0)),
        input_output_aliases={5: 0},
        compiler_params=_cparams(1),
        name="sgu_%d" % chunk_len,
    )(uva, uva, w_s, b_s_t, g_a, mix)


_LOG2E = 1.4426950408889634
_SCALE2 = B_HEAD_DIM ** -0.5 * _LOG2E
_NT = (((1,), (1,)), ((), ()))


def _attn_prompt_kernel(q_ref, *rest):
    k_refs = rest[:ATT_KB]
    v_refs = rest[ATT_KB:2 * ATT_KB]
    bias_ref, gb_ref, _, o_ref, acc_ref = rest[2 * ATT_KB:]
    i = pl.program_id(0)
    for h in range(B_HEADS):
        hs = slice(h * B_HEAD_DIM, (h + 1) * B_HEAD_DIM)
        q = q_ref[:, hs]
        s = []
        for b in range(ATT_KB):
            sb = lax.dot_general(q, k_refs[b][:, hs], _NT,
                                 preferred_element_type=jnp.float32)
            sb = sb + bias_ref[h, :, b * ATT_Q:(b + 1) * ATT_Q]
            if b < ATT_KB - 1:
                sb = jnp.where(i + b >= ATT_KB - 1, sb, NEG)
            s.append(sb)
        m = functools.reduce(jnp.maximum,
                             [jnp.max(sb, axis=-1, keepdims=True) for sb in s])
        p = [jnp.exp2(sb - m) for sb in s]
        l = functools.reduce(lambda a, b: a + b,
                             [jnp.sum(pb, axis=-1, keepdims=True) for pb in p])
        o = functools.reduce(lambda a, b: a + b, [
            jnp.dot(p[b].astype(jnp.bfloat16), v_refs[b][:, hs],
                    preferred_element_type=jnp.float32) for b in range(ATT_KB)])
        acc_ref[:, hs] = o / l
    o = acc_ref[...]
    o_ref[...] = (o * _rms_scale(o) * gb_ref[...]).astype(o_ref.dtype)


def _attn_prompt(qkv16, bias, g_b, mix):
    n_blocks = T_PROMPT // ATT_Q

    def kv_spec(b, col):
        return pl.BlockSpec(
            (ATT_Q, B_WIDTH),
            lambda i: (jnp.maximum(i + b - (ATT_KB - 1), 0), col))

    in_specs = ([pl.BlockSpec((ATT_Q, B_WIDTH), lambda i: (i, 0))]
                + [kv_spec(b, 1) for b in range(ATT_KB)]
                + [kv_spec(b, 2) for b in range(ATT_KB)]
                + [pl.BlockSpec((B_HEADS, ATT_Q, ATT_KB * ATT_Q), lambda i: (0, 0, 0)),
                   pl.BlockSpec((1, B_WIDTH), lambda i: (0, 0)),
                   pl.BlockSpec(memory_space=pl.ANY)])
    n_in = len(in_specs)
    return pl.pallas_call(
        _attn_prompt_kernel,
        out_shape=jax.ShapeDtypeStruct((T_ALL, D_MODEL), jnp.bfloat16),
        grid=(n_blocks,),
        in_specs=in_specs,
        out_specs=pl.BlockSpec((ATT_Q, B_WIDTH), lambda i: (i, 1)),
        scratch_shapes=[pltpu.VMEM((ATT_Q, B_WIDTH), jnp.float32)],
        input_output_aliases={n_in - 1: 0},
        compiler_params=_cparams(1),
        name="attn_prompt",
    )(qkv16, *([qkv16] * (2 * ATT_KB)), bias, g_b, mix)


def _attn_sample_kernel(q_ref, kn_ref, vn_ref, kc_ref, vc_ref, bias_ref, gb_ref, _,
                        o_ref, acc_ref):
    n_cache = kc_ref.shape[0] // B_HEADS
    for h in range(B_HEADS):
        hs = slice(h * B_HEAD_DIM, (h + 1) * B_HEAD_DIM)
        q = q_ref[:, hs]
        kc = kc_ref[pl.ds(h, n_cache, stride=B_HEADS), :].astype(jnp.bfloat16)
        vc = vc_ref[pl.ds(h, n_cache, stride=B_HEADS), :].astype(jnp.bfloat16)
        sc = lax.dot_general(q, kc, _NT, preferred_element_type=jnp.float32)
        sc = sc + bias_ref[h, :, :n_cache]
        sn = lax.dot_general(q, kn_ref[:, hs], _NT, preferred_element_type=jnp.float32)
        sn = sn + bias_ref[h, :, n_cache:n_cache + DEC_SEQ]
        m = jnp.maximum(jnp.max(sc, axis=-1, keepdims=True),
                        jnp.max(sn, axis=-1, keepdims=True))
        pc = jnp.exp2(sc - m)
        pn = jnp.exp2(sn - m)
        l = jnp.sum(pc, axis=-1, keepdims=True) + jnp.sum(pn, axis=-1, keepdims=True)
        o = (jnp.dot(pc.astype(jnp.bfloat16), vc, preferred_element_type=jnp.float32)
             + jnp.dot(pn.astype(jnp.bfloat16), vn_ref[:, hs],
                       preferred_element_type=jnp.float32))
        acc_ref[:, hs] = o / l
    o = acc_ref[...]
    o_ref[...] = (o * _rms_scale(o) * gb_ref[...]).astype(o_ref.dtype)


def _attn_sample(qkv16, cache_k, cache_v, bias, g_b, mix):
    n_cache = cache_k.shape[1] // B_HEADS
    rb0 = T_PROMPT // DEC_SEQ
    new_spec = lambda col: pl.BlockSpec((DEC_SEQ, B_WIDTH), lambda b: (rb0 + b, col))
    cache_spec = pl.BlockSpec((None, n_cache * B_HEADS, B_HEAD_DIM), lambda b: (b, 0, 0))
    return pl.pallas_call(
        _attn_sample_kernel,
        out_shape=jax.ShapeDtypeStruct((T_ALL, D_MODEL), jnp.bfloat16),
        grid=(DEC_BATCH,),
        in_specs=[new_spec(0), new_spec(1), new_spec(2), cache_spec, cache_spec,
                  pl.BlockSpec(bias.shape, lambda b: (0, 0, 0)),
                  pl.BlockSpec((1, B_WIDTH), lambda b: (0, 0)),
                  pl.BlockSpec(memory_space=pl.ANY)],
        out_specs=pl.BlockSpec((DEC_SEQ, B_WIDTH), lambda b: (rb0 + b, 1)),
        scratch_shapes=[pltpu.VMEM((DEC_SEQ, B_WIDTH), jnp.float32)],
        input_output_aliases={7: 0},
        compiler_params=_cparams(1),
        name="attn_sample",
    )(qkv16, qkv16, qkv16, cache_k, cache_v, bias, g_b, mix)


def _rel_bias_log2(table, n_q, n_k, offset):
    width = -(-(n_q + n_k - 2) // LANES) * LANES
    period = width + 1
    t = np.arange(period)
    t = np.where(t < n_k, t, t - period)
    t = np.clip(t, -(n_q - 1), n_k - 1)
    v = table[:, np.clip(offset - t, -(CHUNK - 1), REL_MAX) + (CHUNK - 1)].astype(jnp.float32)
    v = v * _LOG2E
    return jnp.tile(v, (1, n_q))[:, :n_q * width].reshape(-1, n_q, width)


def _prompt_bias(table):
    bias = _rel_bias_log2(table, ATT_Q, ATT_KB * ATT_Q, B_PAST_REACH)
    qc = np.arange(ATT_Q)[:, None] // CHUNK
    kc = np.arange(bias.shape[2])[None, :] // CHUNK
    band = (kc >= qc) & (kc <= qc + B_PAST_REACH // CHUNK)
    return jnp.where(band[None], bias, NEG)


def _split_bf16(x):
    hi = x.astype(jnp.bfloat16)
    lo = (x - hi.astype(jnp.float32)).astype(jnp.bfloat16)
    return hi, lo


def _router_kernel(xp_ref, xs_ref, hp_ref, g_ref, wr_ref, br_ref,
                   h_ref, xt_ref, eid_ref, gate_ref):
    i = pl.program_id(0)

    @pl.when(i < N_PROMPT_TILES)
    def _():
        h_ref[...] = xp_ref[...] + hp_ref[...]

    @pl.when(i >= N_PROMPT_TILES)
    def _():
        h_ref[...] = xs_ref[...] + hp_ref[...]

    h = h_ref[...]
    xt = h * _rms_scale(h) * g_ref[...]
    kd = D_MODEL // LANES
    for k in range(_pitch(kd)):
        piece = (xt[:, k * LANES:(k + 1) * LANES] if k < kd
                 else jnp.zeros((ROW_TILE, LANES), jnp.float32))
        xt_ref[pl.ds(k, ROW_TILE, stride=_pitch(kd)), :] = piece

    xh, xl = _split_bf16(xt)
    wh, wl = _split_bf16(wr_ref[...])
    dot = functools.partial(jnp.dot, preferred_element_type=jnp.float32)
    logits = dot(xh, wh) + (dot(xl, wh) + dot(xh, wl)) + br_ref[...]

    lane = lax.broadcasted_iota(jnp.int32, logits.shape, 1)
    big = jnp.int32(ROUTER_COLS)

    def masked_max(x, mask):
        return jnp.max(jnp.where(mask, x, NEG), axis=-1, keepdims=True)

    def first_lane(mask):
        return jnp.min(jnp.where(mask, lane, big), axis=-1, keepdims=True)

    gmask = lane < N_GROUPS
    gmax = masked_max(logits, gmask)
    gsum = jnp.sum(jnp.where(gmask, jnp.exp(logits - gmax), 0.0), axis=-1, keepdims=True)
    g_sel = first_lane(gmask & (logits == gmax))
    pg_top = 1.0 / gsum

    e_lo = N_GROUPS + g_sel * EXPERTS_PER_GROUP
    emask = (lane >= e_lo) & (lane < e_lo + EXPERTS_PER_GROUP)
    emax = masked_max(logits, emask)
    ex = jnp.where(emask, jnp.exp(logits - emax), 0.0)
    pe = ex / jnp.sum(ex, axis=-1, keepdims=True)
    p1 = jnp.max(pe, axis=-1, keepdims=True)
    l1 = first_lane(emask & (pe == p1))
    rest = emask & (lane != l1)
    p2 = jnp.max(jnp.where(rest, pe, -1.0), axis=-1, keepdims=True)
    l2 = first_lane(rest & (pe == p2))
    denom = p1 + p2
    g1 = pg_top * p1 / denom
    g2 = pg_top * p2 / denom
    eid_ref[...] = jnp.where(lane == 0, l1 - N_GROUPS,
                             jnp.where(lane == 1, l2 - N_GROUPS, 0))
    gate_ref[...] = jnp.where(lane == 0, g1, jnp.where(lane == 1, g2, 0.0))


def _router(xp, xs, hproj, g_ffn, w_router, b_router):
    tile = lambda cols: pl.BlockSpec((ROW_TILE, cols), lambda i: (i, 0))
    return pl.pallas_call(
        _router_kernel,
        out_shape=(jax.ShapeDtypeStruct((T_ALL, D_MODEL), jnp.float32),
                   jax.ShapeDtypeStruct((T_ALL * _pitch(D_MODEL // LANES), LANES), jnp.float32),
                   jax.ShapeDtypeStruct((T_ALL, ROUTER_COLS), jnp.int32),
                   jax.ShapeDtypeStruct((T_ALL, ROUTER_COLS), jnp.float32)),
        grid=(N_ROW_TILES,),
        in_specs=[pl.BlockSpec((ROW_TILE, D_MODEL), _prompt_tile),
                  pl.BlockSpec((ROW_TILE, D_MODEL), _sample_tile),
                  tile(D_MODEL),
                  pl.BlockSpec((1, D_MODEL), lambda i: (0, 0)),
                  pl.BlockSpec((D_MODEL, ROUTER_COLS), lambda i: (0, 0)),
                  pl.BlockSpec((1, ROUTER_COLS), lambda i: (0, 0))],
        out_specs=(tile(D_MODEL),
                   pl.BlockSpec((ROW_TILE * _pitch(D_MODEL // LANES), LANES), lambda i: (i, 0)),
                   tile(ROUTER_COLS), tile(ROUTER_COLS)),
        compiler_params=_cparams(1),
        name="router",
    )(xp, xs, hproj, g_ffn, w_router, b_router)


def _pitch(k):
    return k + 4 if k % 8 == 0 else k


def _row_copy(src_hbm, src_row, buf, slot, dst_row, sem, k):
    p = _pitch(k)
    return pltpu.make_async_copy(src_hbm.at[pl.ds(src_row * p, k)],
                                 buf.at[slot, pl.ds(dst_row * p, k)], sem.at[slot])


def _gather_rows_start(src_hbm, row_of, row0, n_rows, buf, slot, sem, k):
    def body(r, carry):
        _row_copy(src_hbm, row_of(row0 + r), buf, slot, row0 + r, sem, k).start()
        return carry
    lax.fori_loop(0, n_rows, body, 0, unroll=DMA_LOOP_UNROLL)


def _gather_rows_wait(src_hbm, row0, n_rows, buf, slot, sem, k):
    def body(r, carry):
        _row_copy(src_hbm, 0, buf, slot, row0 + r, sem, k).wait()
        return carry
    lax.fori_loop(0, n_rows, body, 0, unroll=DMA_LOOP_UNROLL)


def _rows_2d(buf, slot, row0, m, k):
    p = _pitch(k)
    return jnp.concatenate(
        [buf[slot, pl.ds(row0 * p + kk, m, stride=p), :] for kk in range(k)], axis=-1)


def _row_groups(n_rows, ch, sub, fn):
    for g in range(ch // sub):
        @pl.when(g * sub < n_rows)
        def _():
            fn(g * sub)


def _sort_rows_kernel(cm_ref, cb_ref, nv_ref, tok_ref, xt_hbm, o_ref, xbuf, sem, *, ch, sub):
    c = pl.program_id(0)
    nv = nv_ref[0]
    kd = o_ref.shape[1] // LANES
    last_tok = tok_ref.shape[0] - 1

    def start_gather(cc):
        base = cb_ref[cc]
        _row_groups(cm_ref[cc], ch, sub, lambda r0: _gather_rows_start(
            xt_hbm, lambda r: tok_ref[jnp.minimum(base + r, last_tok)], r0, sub,
            xbuf, cc % 2, sem, kd))

    @pl.when(c == 0)
    def _():
        start_gather(0)

    @pl.when(c + 1 < nv)
    def _():
        start_gather(c + 1)

    @pl.when(c < nv)
    def _():
        slot = c % 2
        _row_groups(cm_ref[c], ch, sub,
                    lambda r0: _gather_rows_wait(xt_hbm, r0, sub, xbuf, slot, sem, kd))
        for m in range(sub, ch + 1, sub):
            @pl.when(cm_ref[c] == m)
            def _():
                o_ref[:m, :] = _rows_2d(xbuf, slot, 0, m, kd).astype(o_ref.dtype)
                if m < ch:
                    o_ref[m:, :] = jnp.zeros((ch - m, o_ref.shape[1]), o_ref.dtype)


def _sort_rows(plan, xt_rows, d_model, *, ch=MOE_CH, sub=MOE_SUB):
    chunk_expert, chunk_mrows, chunk_base, n_chunks, sorted_tok = plan
    nc = chunk_expert.shape[0]
    pd = _pitch(d_model // LANES)
    grid_spec = pltpu.PrefetchScalarGridSpec(
        num_scalar_prefetch=4,
        grid=(nc,),
        in_specs=[pl.BlockSpec(memory_space=pl.ANY)],
        out_specs=pl.BlockSpec((ch, d_model),
                               lambda c, cm, cb, nv, tok: (jnp.minimum(c, nv[0] - 1), 0)),
        scratch_shapes=[pltpu.VMEM((2, ch * pd, LANES), jnp.float32),
                        pltpu.SemaphoreType.DMA((2,))],
    )
    return pl.pallas_call(
        functools.partial(_sort_rows_kernel, ch=ch, sub=sub),
        out_shape=jax.ShapeDtypeStruct((nc * ch, d_model), jnp.bfloat16),
        grid_spec=grid_spec,
        compiler_params=_cparams(1),
        name="sort_rows",
    )(chunk_mrows, chunk_base, n_chunks, sorted_tok, xt_rows)


def _expert_kernel(ce_ref, cm_ref, nv_ref, x_ref, wg_ref, wu_ref, wd_ref, o_ref,
                   *, ch, sub, kslab, slab):
    c = pl.program_id(0)
    j = pl.program_id(1)
    d_model, tf = wg_ref.shape
    kd = d_model // LANES
    pd = _pitch(kd)

    def dot16(a, w):
        return jnp.dot(a, w.astype(jnp.bfloat16), preferred_element_type=jnp.float32)

    @pl.when((c < nv_ref[0]) & (j == 0))
    def _():
        o_ref[...] = jnp.zeros(o_ref.shape, o_ref.dtype)

    @pl.when(c < nv_ref[0])
    def _():
        for m in range(sub, ch + 1, sub):
            @pl.when(cm_ref[c] == m)
            def _():
                x = x_ref[:m, :]
                hg = jnp.zeros((m, tf), jnp.float32)
                hu = jnp.zeros((m, tf), jnp.float32)
                for s in range(d_model // kslab):
                    ks = slice(s * kslab, (s + 1) * kslab)
                    hg = hg + dot16(x[:, ks], wg_ref[ks, :])
                    hu = hu + dot16(x[:, ks], wu_ref[ks, :])
                hh = (jax.nn.silu(hg) * hu).astype(jnp.bfloat16)
                for s in range(d_model // slab):
                    y = dot16(hh, wd_ref[:, s * slab:(s + 1) * slab])
                    for kk in range(slab // LANES):
                        k = s * (slab // LANES) + kk
                        o_ref[pl.ds(k, m, stride=pd), :] += y[:, kk * LANES:(kk + 1) * LANES]


def _experts(plan, x_sorted, w_gate, w_up, w_down, *,
             ch=MOE_CH, sub=MOE_SUB, tf=MOE_TF, kslab=MOE_KSLAB, slab=MOE_SLAB):
    chunk_expert, chunk_mrows, _, n_chunks, _ = plan
    nc = chunk_expert.shape[0]
    d_model, d_expert = w_gate.shape[1], w_gate.shape[2]
    nf = d_expert // tf
    pd = _pitch(d_model // LANES)

    def chunk_of(c, nv):
        return jnp.minimum(c, nv[0] - 1)

    def ftile_of(c, j, nv):
        return jnp.where(c < nv[0], j, nf - 1)

    grid_spec = pltpu.PrefetchScalarGridSpec(
        num_scalar_prefetch=3,
        grid=(nc, nf),
        in_specs=[
            pl.BlockSpec((ch, d_model), lambda c, j, ce, cm, nv: (chunk_of(c, nv), 0)),
            pl.BlockSpec((None, d_model, tf),
                         lambda c, j, ce, cm, nv: (ce[chunk_of(c, nv)], 0, ftile_of(c, j, nv))),
            pl.BlockSpec((None, d_model, tf),
                         lambda c, j, ce, cm, nv: (ce[chunk_of(c, nv)], 0, ftile_of(c, j, nv))),
            pl.BlockSpec((None, tf, d_model),
                         lambda c, j, ce, cm, nv: (ce[chunk_of(c, nv)], ftile_of(c, j, nv), 0)),
        ],
        out_specs=pl.BlockSpec((ch * pd, LANES), lambda c, j, ce, cm, nv: (chunk_of(c, nv), 0)),
    )
    return pl.pallas_call(
        functools.partial(_expert_kernel, ch=ch, sub=sub, kslab=kslab, slab=slab),
        out_shape=jax.ShapeDtypeStruct((nc * ch * pd, LANES), jnp.float32),
        grid_spec=grid_spec,
        compiler_params=_cparams(2),
        name="experts",
    )(chunk_expert, chunk_mrows, n_chunks, x_sorted, w_gate, w_up, w_down)


def _moe_plan(eid, *, ch=MOE_CH, sub=MOE_SUB, nc=MOE_NC, n_experts=N_EXPERTS):
    i32 = jnp.int32
    flat_e = eid.reshape(-1)
    order = jnp.argsort(flat_e, stable=True).astype(i32)
    rank = jnp.argsort(order).astype(i32)
    onehot_e = flat_e[:, None] == jnp.arange(n_experts, dtype=i32)[None, :]
    counts = jnp.sum(onehot_e, axis=0, dtype=i32)
    start = jnp.cumsum(counts) - counts
    nchunk = (counts + ch - 1) // ch
    cend = jnp.cumsum(nchunk)
    cstart = cend - nchunk
    pos = rank + jnp.sum(jnp.where(onehot_e, (cstart * ch - start)[None, :], 0), axis=1)
    n_chunks = cend[-1:]
    cidx = jnp.arange(nc, dtype=i32)
    chunk_expert = jnp.minimum(jnp.sum(cend[None, :] <= cidx[:, None], axis=1, dtype=i32),
                               n_experts - 1)
    onehot_c = chunk_expert[:, None] == jnp.arange(n_experts, dtype=i32)[None, :]
    lookup = lambda table: jnp.sum(jnp.where(onehot_c, table[None, :], 0), axis=1)
    offset = (cidx - lookup(cstart)) * ch
    chunk_rows = jnp.clip(lookup(counts) - offset, 0, ch)
    chunk_rows = jnp.where(cidx < n_chunks[0], chunk_rows, 0)
    chunk_mrows = (chunk_rows + sub - 1) // sub * sub
    chunk_base = lookup(start) + offset
    return (chunk_expert, chunk_mrows.astype(i32), chunk_base.astype(i32),
            n_chunks.astype(i32), order // 2), pos


def _combine_kernel(pos_ref, posn_ref, h_ref, gate_ref, g_ref, y_hbm, op_ref, os_ref,
                    ybuf, sem, *, n_prompt_tiles):
    i = pl.program_id(0)
    n = pl.num_programs(0)
    tile = h_ref.shape[0]
    n_rows = 2 * tile
    kd = h_ref.shape[1] // LANES

    @pl.when(i == 0)
    def _():
        _gather_rows_start(y_hbm, lambda r: pos_ref[0, 0, r], 0, n_rows, ybuf, 0, sem, kd)

    @pl.when(i + 1 < n)
    def _():
        _gather_rows_start(y_hbm, lambda r: posn_ref[0, 0, r], 0, n_rows, ybuf, (i + 1) % 2,
                           sem, kd)

    _gather_rows_wait(y_hbm, 0, n_rows, ybuf, i % 2, sem, kd)
    slot = i % 2
    x = (h_ref[...] + gate_ref[:, 0:1] * _rows_2d(ybuf, slot, 0, tile, kd)
         + gate_ref[:, 1:2] * _rows_2d(ybuf, slot, tile, tile, kd))
    y = x * _rms_scale(x) * g_ref[...]

    @pl.when(i < n_prompt_tiles)
    def _():
        op_ref[...] = y

    @pl.when(i >= n_prompt_tiles)
    def _():
        os_ref[...] = y


def _combine(h, y_rows, pos, gates, g_final, *, t_prompt=T_PROMPT, tile=COMBINE_TILE):
    t_all, d_model = h.shape
    n_tiles = t_all // tile
    n_prompt_tiles = t_prompt // tile
    idx = pos.reshape(n_tiles, tile, 2).transpose(0, 2, 1).reshape(n_tiles, 1, 2 * tile)
    smem = lambda f: pl.BlockSpec((1, 1, 2 * tile), f, memory_space=pltpu.SMEM)
    return pl.pallas_call(
        functools.partial(_combine_kernel, n_prompt_tiles=n_prompt_tiles),
        out_shape=(jax.ShapeDtypeStruct((t_prompt, d_model), jnp.float32),
                   jax.ShapeDtypeStruct((t_all - t_prompt, d_model), jnp.float32)),
        grid=(n_tiles,),
        in_specs=[smem(lambda i: (i, 0, 0)),
                  smem(lambda i: (jnp.minimum(i + 1, n_tiles - 1), 0, 0)),
                  pl.BlockSpec((tile, d_model), lambda i: (i, 0)),
                  pl.BlockSpec((tile, gates.shape[1]), lambda i: (i, 0)),
                  pl.BlockSpec((1, d_model), lambda i: (0, 0)),
                  pl.BlockSpec(memory_space=pl.ANY)],
        out_specs=(pl.BlockSpec((tile, d_model),
                                lambda i: (jnp.minimum(i, n_prompt_tiles - 1), 0)),
                   pl.BlockSpec((tile, d_model),
                                lambda i: (jnp.maximum(i - n_prompt_tiles, 0), 0))),
        scratch_shapes=[pltpu.VMEM((2, 2 * tile * _pitch(d_model // LANES), LANES),
                                   jnp.float32),
                        pltpu.SemaphoreType.DMA((2,))],
        compiler_params=_cparams(1),
        name="combine",
    )(idx, idx, h, gates, g_final, y_rows)


def kernel(x_prompt, x_sample, cache_b_k, cache_b_v, g_mix, w_in, sgu_v_gain, sgu_w, sgu_b,
           rel_bias, g_out_a, g_out_b, w_out, g_ffn, w_router_group, b_router_group,
           w_router_expert, b_router_expert, w_gate, w_up, w_down, g_final):
    assert w_in.shape[0] == 1, "one layer"
    xp = x_prompt.reshape(T_PROMPT, D_MODEL)
    xs = x_sample.reshape(T_SAMPLE, D_MODEL)
    row = lambda v: v.reshape(1, -1)

    xn = _prenorm(xp, xs, row(g_mix[0]))
    w_in16 = w_in[0].astype(jnp.bfloat16)
    uva = _proj_uva(xn, w_in16, row(sgu_v_gain[0]))
    qkv16, qkv32 = _proj_qkv(xn, w_in16)

    w_s, b_s = sgu_w[0], sgu_b[0]
    mix = _sgu(uva, w_s, b_s.T, row(g_out_a[0]), xn,
               chunk_len=A_CHUNK, n_chunks=SGU_CHUNKS_PER_STEP, row0=0,
               n_steps=T_PROMPT // (A_CHUNK * SGU_CHUNKS_PER_STEP))
    mix = _sgu(uva, w_s[:, :DEC_SEQ, :DEC_SEQ], b_s[:, :DEC_SEQ].T, row(g_out_a[0]), mix,
               chunk_len=DEC_SEQ, n_chunks=DEC_BATCH, row0=T_PROMPT, n_steps=1)

    table = rel_bias[0]
    n_cache = cache_b_k.shape[2]
    mix = _attn_prompt(qkv16, _prompt_bias(table), row(g_out_b[0]), mix)
    bias_s = _rel_bias_log2(table, DEC_SEQ, n_cache + DEC_SEQ, n_cache)
    mix = _attn_sample(qkv16, cache_b_k[0].reshape(DEC_BATCH, n_cache * B_HEADS, B_HEAD_DIM),
                       cache_b_v[0].reshape(DEC_BATCH, n_cache * B_HEADS, B_HEAD_DIM),
                       bias_s, row(g_out_b[0]), mix)

    hproj = _proj_out(mix, w_out[0].astype(jnp.bfloat16))

    pad = ROUTER_COLS - N_GROUPS - N_EXPERTS
    w_router = jnp.pad(jnp.concatenate([w_router_group[0], w_router_expert[0]], axis=1),
                       ((0, 0), (0, pad)))
    b_router = jnp.pad(jnp.concatenate([b_router_group[0], b_router_expert[0]]), (0, pad))
    h, xt_rows, eid, gates = _router(xp, xs, hproj, row(g_ffn[0]), w_router, row(b_router))

    plan, pos = _moe_plan(eid[:, :2])
    x_sorted = _sort_rows(plan, xt_rows, D_MODEL)
    y_rows = _experts(plan, x_sorted, w_gate[0], w_up[0], w_down[0])
    y_prompt, y_sample = _combine(h, y_rows, pos, gates, row(g_final))

    keep = min(B_PAST_REACH, SEQ)
    kcols = slice(B_WIDTH, 2 * B_WIDTH)
    vcols = slice(2 * B_WIDTH, 3 * B_WIDTH)
    tail = slice(T_PROMPT - keep, T_PROMPT)
    new = slice(T_PROMPT, T_ALL)
    heads = lambda t, lead: t.reshape(*lead, B_HEADS, B_HEAD_DIM)
    return (y_prompt.reshape(1, SEQ, D_MODEL),
            y_sample.reshape(DEC_BATCH, DEC_SEQ, D_MODEL),
            heads(qkv32[tail, kcols], (1, 1, keep)),
            heads(qkv32[tail, vcols], (1, 1, keep)),
            heads(qkv32[new, kcols], (1, DEC_BATCH, DEC_SEQ)),
            heads(qkv32[new, vcols], (1, DEC_BATCH, DEC_SEQ)),
            uva[new, A_WIDTH:].reshape(1, DEC_BATCH, DEC_SEQ, A_WIDTH))
```

```python
import functools

import numpy as np
import jax
import jax.numpy as jnp
from jax import lax
from jax.experimental import pallas as pl
from jax.experimental.pallas import tpu as pltpu

D_MODEL = 4096
SEQ = 8192
DEC_BATCH = 8
DEC_SEQ = 32
T_PROMPT = SEQ
T_SAMPLE = DEC_BATCH * DEC_SEQ
T_ALL = T_PROMPT + T_SAMPLE

CHUNK = 64
A_WIDTH = D_MODEL // 2
A_GROUPS = 4
A_GROUP_DIM = A_WIDTH // A_GROUPS
A_CHUNK = 128
B_WIDTH = D_MODEL - A_WIDTH
B_HEAD_DIM = 128
B_HEADS = B_WIDTH // B_HEAD_DIM
B_PAST_REACH = 8 * CHUNK
REL_MAX = 256
N_GROUPS = 8
EXPERTS_PER_GROUP = 8
N_EXPERTS = N_GROUPS * EXPERTS_PER_GROUP
D_EXPERT = D_MODEL // 4
EPS = 1e-6
NEG = -1e30

LANES = 128
VMEM_LIMIT_BYTES = 56 * 1024 * 1024
ROW_TILE = 256
COMBINE_TILE = 128
MM_TM = 768
MM_TN = 1024
ATT_Q = 256
ATT_KB = B_PAST_REACH // ATT_Q + 1
SGU_CHUNKS_PER_STEP = 4
MOE_CH = 336
MOE_SUB = 112
MOE_TF = 256
MOE_SLAB = 1024
MOE_NC = (2 * T_ALL) // MOE_CH + N_EXPERTS
ROUTER_COLS = 128
DMA_LOOP_UNROLL = 8

assert T_ALL % MM_TM == 0 and T_PROMPT % ROW_TILE == 0 and T_SAMPLE == ROW_TILE


def _cparams(n_axes):
    return pltpu.CompilerParams(
        dimension_semantics=("arbitrary",) * n_axes,
        vmem_limit_bytes=VMEM_LIMIT_BYTES)


def _rms_scale(x):
    return lax.rsqrt(jnp.mean(x * x, axis=-1, keepdims=True) + EPS)


N_PROMPT_TILES = T_PROMPT // ROW_TILE
N_ROW_TILES = T_ALL // ROW_TILE


def _prenorm_kernel(xp_ref, xs_ref, g_ref, o_ref):
    i = pl.program_id(0)

    def norm(x):
        return (x * _rms_scale(x) * g_ref[...]).astype(o_ref.dtype)

    @pl.when(i < N_PROMPT_TILES)
    def _():
        o_ref[...] = norm(xp_ref[...])

    @pl.when(i >= N_PROMPT_TILES)
    def _():
        o_ref[...] = norm(xs_ref[...])


def _prompt_tile(i):
    return (jnp.minimum(i, N_PROMPT_TILES - 1), 0)


def _sample_tile(i):
    return (jnp.maximum(i - N_PROMPT_TILES, 0), 0)


def _prenorm(xp, xs, g):
    return pl.pallas_call(
        _prenorm_kernel,
        out_shape=jax.ShapeDtypeStruct((T_ALL, D_MODEL), jnp.bfloat16),
        grid=(N_ROW_TILES,),
        in_specs=[pl.BlockSpec((ROW_TILE, D_MODEL), _prompt_tile),
                  pl.BlockSpec((ROW_TILE, D_MODEL), _sample_tile),
                  pl.BlockSpec((1, D_MODEL), lambda i: (0, 0))],
        out_specs=pl.BlockSpec((ROW_TILE, D_MODEL), lambda i: (i, 0)),
        compiler_params=_cparams(1),
        name="prenorm",
    )(xp, xs, g)


def _uva_kernel(x_ref, w_ref, vg_ref, o_ref):
    j = pl.program_id(0)
    acc = jnp.dot(x_ref[...], w_ref[...], preferred_element_type=jnp.float32)
    g = jax.nn.gelu(acc)

    @pl.when(j < A_WIDTH // MM_TN)
    def _():
        o_ref[...] = g

    @pl.when(j >= A_WIDTH // MM_TN)
    def _():
        for k in range(MM_TN // A_GROUP_DIM):
            cols = slice(k * A_GROUP_DIM, (k + 1) * A_GROUP_DIM)
            gg = g[:, cols]
            o_ref[:, cols] = gg * _rms_scale(gg) * vg_ref[:, cols]


def _qkv_kernel(x_ref, w_ref, o16_ref, o32_ref):
    acc = jnp.dot(x_ref[...], w_ref[...], preferred_element_type=jnp.float32)
    o16_ref[...] = acc.astype(o16_ref.dtype)
    o32_ref[...] = acc


def _plain_mm_kernel(x_ref, w_ref, o_ref):
    o_ref[...] = jnp.dot(x_ref[...], w_ref[...], preferred_element_type=jnp.float32)


def _proj_uva(xn, w_in16, v_gain):
    n_va_tiles = A_WIDTH // MM_TN
    return pl.pallas_call(
        _uva_kernel,
        out_shape=jax.ShapeDtypeStruct((T_ALL, 2 * A_WIDTH), jnp.float32),
        grid=(2 * A_WIDTH // MM_TN, T_ALL // MM_TM),
        in_specs=[pl.BlockSpec((MM_TM, D_MODEL), lambda j, i: (i, 0)),
                  pl.BlockSpec((D_MODEL, MM_TN), lambda j, i: (0, j)),
                  pl.BlockSpec((1, MM_TN), lambda j, i: (0, jnp.maximum(j - n_va_tiles, 0)))],
        out_specs=pl.BlockSpec((MM_TM, MM_TN), lambda j, i: (i, j)),
        compiler_params=_cparams(2),
        name="proj_uva",
    )(xn, w_in16, v_gain)


def _proj_qkv(xn, w_in16):
    col0 = 2 * A_WIDTH // MM_TN
    n_cols = 3 * B_WIDTH
    shape = (T_ALL, n_cols)
    return pl.pallas_call(
        _qkv_kernel,
        out_shape=(jax.ShapeDtypeStruct(shape, jnp.bfloat16),
                   jax.ShapeDtypeStruct(shape, jnp.float32)),
        grid=(n_cols // MM_TN, T_ALL // MM_TM),
        in_specs=[pl.BlockSpec((MM_TM, D_MODEL), lambda j, i: (i, 0)),
                  pl.BlockSpec((D_MODEL, MM_TN), lambda j, i: (0, col0 + j))],
        out_specs=(pl.BlockSpec((MM_TM, MM_TN), lambda j, i: (i, j)),
                   pl.BlockSpec((MM_TM, MM_TN), lambda j, i: (i, j))),
        compiler_params=_cparams(2),
        name="proj_qkv",
    )(xn, w_in16)


def _proj_out(mix, w_out16):
    return pl.pallas_call(
        _plain_mm_kernel,
        out_shape=jax.ShapeDtypeStruct((T_ALL, D_MODEL), jnp.float32),
        grid=(D_MODEL // MM_TN, T_ALL // MM_TM),
        in_specs=[pl.BlockSpec((MM_TM, D_MODEL), lambda j, i: (i, 0)),
                  pl.BlockSpec((D_MODEL, MM_TN), lambda j, i: (0, j))],
        out_specs=pl.BlockSpec((MM_TM, MM_TN), lambda j, i: (i, j)),
        compiler_params=_cparams(2),
        name="proj_out",
    )(mix, w_out16)


def _sgu_kernel(u_ref, v_ref, w_ref, b_ref, ga_ref, *rest, chunk_len, n_chunks):
    o_ref = rest[-1]
    L = chunk_len
    row = lax.broadcasted_iota(jnp.int32, (L, L), 0)
    col = lax.broadcasted_iota(jnp.int32, (L, L), 1)
    w16 = [jnp.where(col <= row, w_ref[g], 0.0).astype(jnp.bfloat16)
           for g in range(A_GROUPS)]
    for c in range(n_chunks):
        rows = slice(c * L, (c + 1) * L)
        parts = []
        ss = jnp.zeros((L, 1), jnp.float32)
        for g in range(A_GROUPS):
            cols = slice(g * A_GROUP_DIM, (g + 1) * A_GROUP_DIM)
            f = jnp.dot(w16[g], v_ref[rows, cols].astype(jnp.bfloat16),
                        preferred_element_type=jnp.float32) + b_ref[:, g:g + 1]
            a = u_ref[rows, cols] * f
            parts.append(a)
            ss = ss + jnp.sum(a * a, axis=-1, keepdims=True)
        r = lax.rsqrt(ss * (1.0 / A_WIDTH) + EPS)
        for g in range(A_GROUPS):
            cols = slice(g * A_GROUP_DIM, (g + 1) * A_GROUP_DIM)
            o_ref[rows, cols] = (parts[g] * r * ga_ref[:, cols]).astype(o_ref.dtype)


def _sgu(uva, w_s, b_s_t, g_a, mix, *, chunk_len, n_chunks, row0, n_steps):
    rows = chunk_len * n_chunks
    rb0 = row0 // rows
    kernel = functools.partial(_sgu_kernel, chunk_len=chunk_len, n_chunks=n_chunks)
    return pl.pallas_call(
        kernel,
        out_shape=jax.ShapeDtypeStruct((T_ALL, D_MODEL), jnp.bfloat16),
        grid=(n_steps,),
        in_specs=[pl.BlockSpec((rows, A_WIDTH), lambda i: (rb0 + i, 0)),
                  pl.BlockSpec((rows, A_WIDTH), lambda i: (rb0 + i, 1)),
                  pl.BlockSpec((A_GROUPS, chunk_len, chunk_len), lambda i: (0, 0, 0)),
                  pl.BlockSpec((chunk_len, A_GROUPS), lambda i: (0, 0)),
                  pl.BlockSpec((1, A_WIDTH), lambda i: (0, 0)),
                  pl.BlockSpec(memory_space=pl.ANY)],
        out_specs=pl.BlockSpec((rows, A_WIDTH), lambda i: (rb0 + i, 0)),
        input_output_aliases={5: 0},
        compiler_params=_cparams(1),
        name="sgu_%d" % chunk_len,
    )(uva, uva, w_s, b_s_t, g_a, mix)


_LOG2E = 1.4426950408889634
_SCALE2 = B_HEAD_DIM ** -0.5 * _LOG2E
_NT = (((1,), (1,)), ((), ()))


def _attn_prompt_kernel(q_ref, *rest):
    k_refs = rest[:ATT_KB]
    v_refs = rest[ATT_KB:2 * ATT_KB]
    bias_ref, gb_ref, _, o_ref, acc_ref = rest[2 * ATT_KB:]
    i = pl.program_id(0)
    for h in range(B_HEADS):
        hs = slice(h * B_HEAD_DIM, (h + 1) * B_HEAD_DIM)
        q = q_ref[:, hs]
        s = []
        for b in range(ATT_KB):
            sb = lax.dot_general(q, k_refs[b][:, hs], _NT,
                                 preferred_element_type=jnp.float32)
            sb = sb * _SCALE2 + bias_ref[h, :, b * ATT_Q:(b + 1) * ATT_Q]
            if b < ATT_KB - 1:
                sb = jnp.where(i + b >= ATT_KB - 1, sb, NEG)
            s.append(sb)
        m = functools.reduce(jnp.maximum,
                             [jnp.max(sb, axis=-1, keepdims=True) for sb in s])
        p = [jnp.exp2(sb - m) for sb in s]
        l = functools.reduce(lambda a, b: a + b,
                             [jnp.sum(pb, axis=-1, keepdims=True) for pb in p])
        o = functools.reduce(lambda a, b: a + b, [
            jnp.dot(p[b].astype(jnp.bfloat16), v_refs[b][:, hs],
                    preferred_element_type=jnp.float32) for b in range(ATT_KB)])
        acc_ref[:, hs] = o / l
    o = acc_ref[...]
    o_ref[...] = (o * _rms_scale(o) * gb_ref[...]).astype(o_ref.dtype)


def _attn_prompt(qkv16, bias, g_b, mix):
    n_blocks = T_PROMPT // ATT_Q

    def kv_spec(b, col):
        return pl.BlockSpec(
            (ATT_Q, B_WIDTH),
            lambda i: (jnp.maximum(i + b - (ATT_KB - 1), 0), col))

    in_specs = ([pl.BlockSpec((ATT_Q, B_WIDTH), lambda i: (i, 0))]
                + [kv_spec(b, 1) for b in range(ATT_KB)]
                + [kv_spec(b, 2) for b in range(ATT_KB)]
                + [pl.BlockSpec((B_HEADS, ATT_Q, ATT_KB * ATT_Q), lambda i: (0, 0, 0)),
                   pl.BlockSpec((1, B_WIDTH), lambda i: (0, 0)),
                   pl.BlockSpec(memory_space=pl.ANY)])
    n_in = len(in_specs)
    return pl.pallas_call(
        _attn_prompt_kernel,
        out_shape=jax.ShapeDtypeStruct((T_ALL, D_MODEL), jnp.bfloat16),
        grid=(n_blocks,),
        in_specs=in_specs,
        out_specs=pl.BlockSpec((ATT_Q, B_WIDTH), lambda i: (i, 1)),
        scratch_shapes=[pltpu.VMEM((ATT_Q, B_WIDTH), jnp.float32)],
        input_output_aliases={n_in - 1: 0},
        compiler_params=_cparams(1),
        name="attn_prompt",
    )(qkv16, *([qkv16] * (2 * ATT_KB)), bias, g_b, mix)


def _attn_sample_kernel(q_ref, kn_ref, vn_ref, kc_ref, vc_ref, bias_ref, gb_ref, _,
                        o_ref, acc_ref):
    n_cache = kc_ref.shape[0] // B_HEADS
    for h in range(B_HEADS):
        hs = slice(h * B_HEAD_DIM, (h + 1) * B_HEAD_DIM)
        q = q_ref[:, hs]
        kc = kc_ref[pl.ds(h, n_cache, stride=B_HEADS), :].astype(jnp.bfloat16)
        vc = vc_ref[pl.ds(h, n_cache, stride=B_HEADS), :].astype(jnp.bfloat16)
        sc = lax.dot_general(q, kc, _NT, preferred_element_type=jnp.float32)
        sc = sc * _SCALE2 + bias_ref[h, :, :n_cache]
        sn = lax.dot_general(q, kn_ref[:, hs], _NT, preferred_element_type=jnp.float32)
        sn = sn * _SCALE2 + bias_ref[h, :, n_cache:]
        m = jnp.maximum(jnp.max(sc, axis=-1, keepdims=True),
                        jnp.max(sn, axis=-1, keepdims=True))
        pc = jnp.exp2(sc - m)
        pn = jnp.exp2(sn - m)
        l = jnp.sum(pc, axis=-1, keepdims=True) + jnp.sum(pn, axis=-1, keepdims=True)
        o = (jnp.dot(pc.astype(jnp.bfloat16), vc, preferred_element_type=jnp.float32)
             + jnp.dot(pn.astype(jnp.bfloat16), vn_ref[:, hs],
                       preferred_element_type=jnp.float32))
        acc_ref[:, hs] = o / l
    o = acc_ref[...]
    o_ref[...] = (o * _rms_scale(o) * gb_ref[...]).astype(o_ref.dtype)


def _attn_sample(qkv16, cache_k, cache_v, bias, g_b, mix):
    n_cache = cache_k.shape[1] // B_HEADS
    rb0 = T_PROMPT // DEC_SEQ
    new_spec = lambda col: pl.BlockSpec((DEC_SEQ, B_WIDTH), lambda b: (rb0 + b, col))
    cache_spec = pl.BlockSpec((None, n_cache * B_HEADS, B_HEAD_DIM), lambda b: (b, 0, 0))
    return pl.pallas_call(
        _attn_sample_kernel,
        out_shape=jax.ShapeDtypeStruct((T_ALL, D_MODEL), jnp.bfloat16),
        grid=(DEC_BATCH,),
        in_specs=[new_spec(0), new_spec(1), new_spec(2), cache_spec, cache_spec,
                  pl.BlockSpec((B_HEADS, DEC_SEQ, n_cache + DEC_SEQ), lambda b: (0, 0, 0)),
                  pl.BlockSpec((1, B_WIDTH), lambda b: (0, 0)),
                  pl.BlockSpec(memory_space=pl.ANY)],
        out_specs=pl.BlockSpec((DEC_SEQ, B_WIDTH), lambda b: (rb0 + b, 1)),
        scratch_shapes=[pltpu.VMEM((DEC_SEQ, B_WIDTH), jnp.float32)],
        input_output_aliases={7: 0},
        compiler_params=_cparams(1),
        name="attn_sample",
    )(qkv16, qkv16, qkv16, cache_k, cache_v, bias, g_b, mix)


def _rel_bias_log2(table, n_q, n_k, offset):
    n = n_q + n_k - 1
    t = np.concatenate([np.arange(0, n_k), np.arange(-(n_q - 1), 0)])
    v = table[:, np.clip(offset - t, -(CHUNK - 1), REL_MAX) + (CHUNK - 1)].astype(jnp.float32)
    v = v * _LOG2E
    toeplitz = jnp.tile(v, (1, n_q))[:, :n_q * (n - 1)].reshape(-1, n_q, n - 1)
    return toeplitz[:, :, :n_k]


def _prompt_bias(table):
    n_k = ATT_KB * ATT_Q
    bias = _rel_bias_log2(table, ATT_Q, n_k, B_PAST_REACH)
    qc = np.arange(ATT_Q)[:, None] // CHUNK
    kc = np.arange(n_k)[None, :] // CHUNK
    band = (kc >= qc) & (kc <= qc + B_PAST_REACH // CHUNK)
    return jnp.where(band[None], bias, NEG)


def _split_bf16(x):
    hi = x.astype(jnp.bfloat16)
    lo = (x - hi.astype(jnp.float32)).astype(jnp.bfloat16)
    return hi, lo


def _router_kernel(xp_ref, xs_ref, hp_ref, g_ref, wr_ref, br_ref,
                   h_ref, xt_ref, eid_ref, gate_ref):
    i = pl.program_id(0)

    @pl.when(i < N_PROMPT_TILES)
    def _():
        h_ref[...] = xp_ref[...] + hp_ref[...]

    @pl.when(i >= N_PROMPT_TILES)
    def _():
        h_ref[...] = xs_ref[...] + hp_ref[...]

    h = h_ref[...]
    xt = h * _rms_scale(h) * g_ref[...]
    kd = D_MODEL // LANES
    for k in range(_pitch(kd)):
        piece = (xt[:, k * LANES:(k + 1) * LANES] if k < kd
                 else jnp.zeros((ROW_TILE, LANES), jnp.float32))
        xt_ref[pl.ds(k, ROW_TILE, stride=_pitch(kd)), :] = piece

    xh, xl = _split_bf16(xt)
    wh, wl = _split_bf16(wr_ref[...])
    dot = functools.partial(jnp.dot, preferred_element_type=jnp.float32)
    logits = dot(xh, wh) + (dot(xl, wh) + dot(xh, wl)) + br_ref[...]

    lane = lax.broadcasted_iota(jnp.int32, logits.shape, 1)
    big = jnp.int32(ROUTER_COLS)

    def masked_max(x, mask):
        return jnp.max(jnp.where(mask, x, NEG), axis=-1, keepdims=True)

    def first_lane(mask):
        return jnp.min(jnp.where(mask, lane, big), axis=-1, keepdims=True)

    gmask = lane < N_GROUPS
    gmax = masked_max(logits, gmask)
    gsum = jnp.sum(jnp.where(gmask, jnp.exp(logits - gmax), 0.0), axis=-1, keepdims=True)
    g_sel = first_lane(gmask & (logits == gmax))
    pg_top = 1.0 / gsum

    e_lo = N_GROUPS + g_sel * EXPERTS_PER_GROUP
    emask = (lane >= e_lo) & (lane < e_lo + EXPERTS_PER_GROUP)
    emax = masked_max(logits, emask)
    ex = jnp.where(emask, jnp.exp(logits - emax), 0.0)
    pe = ex / jnp.sum(ex, axis=-1, keepdims=True)
    p1 = jnp.max(pe, axis=-1, keepdims=True)
    l1 = first_lane(emask & (pe == p1))
    rest = emask & (lane != l1)
    p2 = jnp.max(jnp.where(rest, pe, -1.0), axis=-1, keepdims=True)
    l2 = first_lane(rest & (pe == p2))
    denom = p1 + p2
    g1 = pg_top * p1 / denom
    g2 = pg_top * p2 / denom
    eid_ref[...] = jnp.where(lane == 0, l1 - N_GROUPS,
                             jnp.where(lane == 1, l2 - N_GROUPS, 0))
    gate_ref[...] = jnp.where(lane == 0, g1, jnp.where(lane == 1, g2, 0.0))


def _router(xp, xs, hproj, g_ffn, w_router, b_router):
    tile = lambda cols: pl.BlockSpec((ROW_TILE, cols), lambda i: (i, 0))
    return pl.pallas_call(
        _router_kernel,
        out_shape=(jax.ShapeDtypeStruct((T_ALL, D_MODEL), jnp.float32),
                   jax.ShapeDtypeStruct((T_ALL * _pitch(D_MODEL // LANES), LANES), jnp.float32),
                   jax.ShapeDtypeStruct((T_ALL, ROUTER_COLS), jnp.int32),
                   jax.ShapeDtypeStruct((T_ALL, ROUTER_COLS), jnp.float32)),
        grid=(N_ROW_TILES,),
        in_specs=[pl.BlockSpec((ROW_TILE, D_MODEL), _prompt_tile),
                  pl.BlockSpec((ROW_TILE, D_MODEL), _sample_tile),
                  tile(D_MODEL),
                  pl.BlockSpec((1, D_MODEL), lambda i: (0, 0)),
                  pl.BlockSpec((D_MODEL, ROUTER_COLS), lambda i: (0, 0)),
                  pl.BlockSpec((1, ROUTER_COLS), lambda i: (0, 0))],
        out_specs=(tile(D_MODEL),
                   pl.BlockSpec((ROW_TILE * _pitch(D_MODEL // LANES), LANES), lambda i: (i, 0)),
                   tile(ROUTER_COLS), tile(ROUTER_COLS)),
        compiler_params=_cparams(1),
        name="router",
    )(xp, xs, hproj, g_ffn, w_router, b_router)


def _pitch(k):
    return k + 4 if k % 8 == 0 else k


def _row_copy(src_hbm, src_row, buf, slot, dst_row, sem, k):
    p = _pitch(k)
    return pltpu.make_async_copy(src_hbm.at[pl.ds(src_row * p, k)],
                                 buf.at[slot, pl.ds(dst_row * p, k)], sem.at[slot])


def _gather_rows_start(src_hbm, row_of, row0, n_rows, buf, slot, sem, k):
    def body(r, carry):
        _row_copy(src_hbm, row_of(row0 + r), buf, slot, row0 + r, sem, k).start()
        return carry
    lax.fori_loop(0, n_rows, body, 0, unroll=DMA_LOOP_UNROLL)


def _gather_rows_wait(src_hbm, row0, n_rows, buf, slot, sem, k):
    def body(r, carry):
        _row_copy(src_hbm, 0, buf, slot, row0 + r, sem, k).wait()
        return carry
    lax.fori_loop(0, n_rows, body, 0, unroll=DMA_LOOP_UNROLL)


def _rows_2d(buf, slot, row0, m, k):
    p = _pitch(k)
    return jnp.concatenate(
        [buf[slot, pl.ds(row0 * p + kk, m, stride=p), :] for kk in range(k)], axis=-1)


def _row_groups(n_rows, ch, sub, fn):
    for g in range(ch // sub):
        @pl.when(g * sub < n_rows)
        def _():
            fn(g * sub)


def _sort_rows_kernel(cm_ref, cb_ref, nv_ref, tok_ref, xt_hbm, o_ref, xbuf, sem, *, ch, sub):
    c = pl.program_id(0)
    nv = nv_ref[0]
    kd = o_ref.shape[1] // LANES
    last_tok = tok_ref.shape[0] - 1

    def start_gather(cc):
        base = cb_ref[cc]
        _row_groups(cm_ref[cc], ch, sub, lambda r0: _gather_rows_start(
            xt_hbm, lambda r: tok_ref[jnp.minimum(base + r, last_tok)], r0, sub,
            xbuf, cc % 2, sem, kd))

    @pl.when(c == 0)
    def _():
        start_gather(0)

    @pl.when(c + 1 < nv)
    def _():
        start_gather(c + 1)

    @pl.when(c < nv)
    def _():
        slot = c % 2
        _row_groups(cm_ref[c], ch, sub,
                    lambda r0: _gather_rows_wait(xt_hbm, r0, sub, xbuf, slot, sem, kd))
        for m in range(sub, ch + 1, sub):
            @pl.when(cm_ref[c] == m)
            def _():
                o_ref[:m, :] = _rows_2d(xbuf, slot, 0, m, kd).astype(o_ref.dtype)
                if m < ch:
                    o_ref[m:, :] = jnp.zeros((ch - m, o_ref.shape[1]), o_ref.dtype)


def _sort_rows(plan, xt_rows, d_model, *, ch=MOE_CH, sub=MOE_SUB):
    chunk_expert, chunk_mrows, chunk_base, n_chunks, sorted_tok = plan
    nc = chunk_expert.shape[0]
    pd = _pitch(d_model // LANES)
    grid_spec = pltpu.PrefetchScalarGridSpec(
        num_scalar_prefetch=4,
        grid=(nc,),
        in_specs=[pl.BlockSpec(memory_space=pl.ANY)],
        out_specs=pl.BlockSpec((ch, d_model),
                               lambda c, cm, cb, nv, tok: (jnp.minimum(c, nv[0] - 1), 0)),
        scratch_shapes=[pltpu.VMEM((2, ch * pd, LANES), jnp.float32),
                        pltpu.SemaphoreType.DMA((2,))],
    )
    return pl.pallas_call(
        functools.partial(_sort_rows_kernel, ch=ch, sub=sub),
        out_shape=jax.ShapeDtypeStruct((nc * ch, d_model), jnp.bfloat16),
        grid_spec=grid_spec,
        compiler_params=_cparams(1),
        name="sort_rows",
    )(chunk_mrows, chunk_base, n_chunks, sorted_tok, xt_rows)


def _expert_kernel(ce_ref, cm_ref, nv_ref, x_ref, wg_ref, wu_ref, wd_ref, o_ref,
                   *, ch, sub, slab):
    c = pl.program_id(0)
    j = pl.program_id(1)
    d_model, tf = wg_ref.shape
    kd = d_model // LANES
    pd = _pitch(kd)

    def dot16(a, w):
        return jnp.dot(a, w.astype(jnp.bfloat16), preferred_element_type=jnp.float32)

    @pl.when((c < nv_ref[0]) & (j == 0))
    def _():
        o_ref[...] = jnp.zeros(o_ref.shape, o_ref.dtype)

    @pl.when(c < nv_ref[0])
    def _():
        for m in range(sub, ch + 1, sub):
            @pl.when(cm_ref[c] == m)
            def _():
                x = x_ref[:m, :]
                hg = jnp.zeros((m, tf), jnp.float32)
                hu = jnp.zeros((m, tf), jnp.float32)
                for s in range(d_model // slab):
                    ks = slice(s * slab, (s + 1) * slab)
                    hg = hg + dot16(x[:, ks], wg_ref[ks, :])
                    hu = hu + dot16(x[:, ks], wu_ref[ks, :])
                hh = (jax.nn.silu(hg) * hu).astype(jnp.bfloat16)
                for s in range(d_model // slab):
                    y = dot16(hh, wd_ref[:, s * slab:(s + 1) * slab])
                    for kk in range(slab // LANES):
                        k = s * (slab // LANES) + kk
                        o_ref[pl.ds(k, m, stride=pd), :] += y[:, kk * LANES:(kk + 1) * LANES]


def _experts(plan, x_sorted, w_gate, w_up, w_down, *,
             ch=MOE_CH, sub=MOE_SUB, tf=MOE_TF, slab=MOE_SLAB):
    chunk_expert, chunk_mrows, _, n_chunks, _ = plan
    nc = chunk_expert.shape[0]
    d_model, d_expert = w_gate.shape[1], w_gate.shape[2]
    nf = d_expert // tf
    pd = _pitch(d_model // LANES)

    def chunk_of(c, nv):
        return jnp.minimum(c, nv[0] - 1)

    def ftile_of(c, j, nv):
        return jnp.where(c < nv[0], j, nf - 1)

    grid_spec = pltpu.PrefetchScalarGridSpec(
        num_scalar_prefetch=3,
        grid=(nc, nf),
        in_specs=[
            pl.BlockSpec((ch, d_model), lambda c, j, ce, cm, nv: (chunk_of(c, nv), 0)),
            pl.BlockSpec((None, d_model, tf),
                         lambda c, j, ce, cm, nv: (ce[chunk_of(c, nv)], 0, ftile_of(c, j, nv))),
            pl.BlockSpec((None, d_model, tf),
                         lambda c, j, ce, cm, nv: (ce[chunk_of(c, nv)], 0, ftile_of(c, j, nv))),
            pl.BlockSpec((None, tf, d_model),
                         lambda c, j, ce, cm, nv: (ce[chunk_of(c, nv)], ftile_of(c, j, nv), 0)),
        ],
        out_specs=pl.BlockSpec((ch * pd, LANES), lambda c, j, ce, cm, nv: (chunk_of(c, nv), 0)),
    )
    return pl.pallas_call(
        functools.partial(_expert_kernel, ch=ch, sub=sub, slab=slab),
        out_shape=jax.ShapeDtypeStruct((nc * ch * pd, LANES), jnp.float32),
        grid_spec=grid_spec,
        compiler_params=_cparams(2),
        name="experts",
    )(chunk_expert, chunk_mrows, n_chunks, x_sorted, w_gate, w_up, w_down)


def _moe_plan(eid, *, ch=MOE_CH, sub=MOE_SUB, nc=MOE_NC, n_experts=N_EXPERTS):
    i32 = jnp.int32
    flat_e = eid.reshape(-1)
    order = jnp.argsort(flat_e, stable=True).astype(i32)
    rank = jnp.argsort(order).astype(i32)
    onehot_e = flat_e[:, None] == jnp.arange(n_experts, dtype=i32)[None, :]
    counts = jnp.sum(onehot_e, axis=0, dtype=i32)
    start = jnp.cumsum(counts) - counts
    nchunk = (counts + ch - 1) // ch
    cend = jnp.cumsum(nchunk)
    cstart = cend - nchunk
    pos = rank + jnp.sum(jnp.where(onehot_e, (cstart * ch - start)[None, :], 0), axis=1)
    n_chunks = cend[-1:]
    cidx = jnp.arange(nc, dtype=i32)
    chunk_expert = jnp.minimum(jnp.sum(cend[None, :] <= cidx[:, None], axis=1, dtype=i32),
                               n_experts - 1)
    onehot_c = chunk_expert[:, None] == jnp.arange(n_experts, dtype=i32)[None, :]
    lookup = lambda table: jnp.sum(jnp.where(onehot_c, table[None, :], 0), axis=1)
    offset = (cidx - lookup(cstart)) * ch
    chunk_rows = jnp.clip(lookup(counts) - offset, 0, ch)
    chunk_rows = jnp.where(cidx < n_chunks[0], chunk_rows, 0)
    chunk_mrows = (chunk_rows + sub - 1) // sub * sub
    chunk_base = lookup(start) + offset
    return (chunk_expert, chunk_mrows.astype(i32), chunk_base.astype(i32),
            n_chunks.astype(i32), order // 2), pos


def _combine_kernel(pos_ref, posn_ref, h_ref, gate_ref, g_ref, y_hbm, op_ref, os_ref,
                    ybuf, sem, *, n_prompt_tiles):
    i = pl.program_id(0)
    n = pl.num_programs(0)
    tile = h_ref.shape[0]
    n_rows = 2 * tile
    kd = h_ref.shape[1] // LANES

    @pl.when(i == 0)
    def _():
        _gather_rows_start(y_hbm, lambda r: pos_ref[0, 0, r], 0, n_rows, ybuf, 0, sem, kd)

    @pl.when(i + 1 < n)
    def _():
        _gather_rows_start(y_hbm, lambda r: posn_ref[0, 0, r], 0, n_rows, ybuf, (i + 1) % 2,
                           sem, kd)

    _gather_rows_wait(y_hbm, 0, n_rows, ybuf, i % 2, sem, kd)
    slot = i % 2
    x = (h_ref[...] + gate_ref[:, 0:1] * _rows_2d(ybuf, slot, 0, tile, kd)
         + gate_ref[:, 1:2] * _rows_2d(ybuf, slot, tile, tile, kd))
    y = x * _rms_scale(x) * g_ref[...]

    @pl.when(i < n_prompt_tiles)
    def _():
        op_ref[...] = y

    @pl.when(i >= n_prompt_tiles)
    def _():
        os_ref[...] = y


def _combine(h, y_rows, pos, gates, g_final, *, t_prompt=T_PROMPT, tile=COMBINE_TILE):
    t_all, d_model = h.shape
    n_tiles = t_all // tile
    n_prompt_tiles = t_prompt // tile
    idx = pos.reshape(n_tiles, tile, 2).transpose(0, 2, 1).reshape(n_tiles, 1, 2 * tile)
    smem = lambda f: pl.BlockSpec((1, 1, 2 * tile), f, memory_space=pltpu.SMEM)
    return pl.pallas_call(
        functools.partial(_combine_kernel, n_prompt_tiles=n_prompt_tiles),
        out_shape=(jax.ShapeDtypeStruct((t_prompt, d_model), jnp.float32),
                   jax.ShapeDtypeStruct((t_all - t_prompt, d_model), jnp.float32)),
        grid=(n_tiles,),
        in_specs=[smem(lambda i: (i, 0, 0)),
                  smem(lambda i: (jnp.minimum(i + 1, n_tiles - 1), 0, 0)),
                  pl.BlockSpec((tile, d_model), lambda i: (i, 0)),
                  pl.BlockSpec((tile, gates.shape[1]), lambda i: (i, 0)),
                  pl.BlockSpec((1, d_model), lambda i: (0, 0)),
                  pl.BlockSpec(memory_space=pl.ANY)],
        out_specs=(pl.BlockSpec((tile, d_model),
                                lambda i: (jnp.minimum(i, n_prompt_tiles - 1), 0)),
                   pl.BlockSpec((tile, d_model),
                                lambda i: (jnp.maximum(i - n_prompt_tiles, 0), 0))),
        scratch_shapes=[pltpu.VMEM((2, 2 * tile * _pitch(d_model // LANES), LANES),
                                   jnp.float32),
                        pltpu.SemaphoreType.DMA((2,))],
        compiler_params=_cparams(1),
        name="combine",
    )(idx, idx, h, gates, g_final, y_rows)


def kernel(x_prompt, x_sample, cache_b_k, cache_b_v, g_mix, w_in, sgu_v_gain, sgu_w, sgu_b,
           rel_bias, g_out_a, g_out_b, w_out, g_ffn, w_router_group, b_router_group,
           w_router_expert, b_router_expert, w_gate, w_up, w_down, g_final):
    assert w_in.shape[0] == 1, "one layer"
    xp = x_prompt.reshape(T_PROMPT, D_MODEL)
    xs = x_sample.reshape(T_SAMPLE, D_MODEL)
    row = lambda v: v.reshape(1, -1)

    xn = _prenorm(xp, xs, row(g_mix[0]))
    w_in16 = w_in[0].astype(jnp.bfloat16)
    uva = _proj_uva(xn, w_in16, row(sgu_v_gain[0]))
    qkv16, qkv32 = _proj_qkv(xn, w_in16)

    w_s, b_s = sgu_w[0], sgu_b[0]
    mix = _sgu(uva, w_s, b_s.T, row(g_out_a[0]), xn,
               chunk_len=A_CHUNK, n_chunks=SGU_CHUNKS_PER_STEP, row0=0,
               n_steps=T_PROMPT // (A_CHUNK * SGU_CHUNKS_PER_STEP))
    mix = _sgu(uva, w_s[:, :DEC_SEQ, :DEC_SEQ], b_s[:, :DEC_SEQ].T, row(g_out_a[0]), mix,
               chunk_len=DEC_SEQ, n_chunks=DEC_BATCH, row0=T_PROMPT, n_steps=1)

    table = rel_bias[0]
    n_cache = cache_b_k.shape[2]
    mix = _attn_prompt(qkv16, _prompt_bias(table), row(g_out_b[0]), mix)
    bias_s = _rel_bias_log2(table, DEC_SEQ, n_cache + DEC_SEQ, n_cache)
    mix = _attn_sample(qkv16, cache_b_k[0].reshape(DEC_BATCH, n_cache * B_HEADS, B_HEAD_DIM),
                       cache_b_v[0].reshape(DEC_BATCH, n_cache * B_HEADS, B_HEAD_DIM),
                       bias_s, row(g_out_b[0]), mix)

    hproj = _proj_out(mix, w_out[0].astype(jnp.bfloat16))

    pad = ROUTER_COLS - N_GROUPS - N_EXPERTS
    w_router = jnp.pad(jnp.concatenate([w_router_group[0], w_router_expert[0]], axis=1),
                       ((0, 0), (0, pad)))
    b_router = jnp.pad(jnp.concatenate([b_router_group[0], b_router_expert[0]]), (0, pad))
    h, xt_rows, eid, gates = _router(xp, xs, hproj, row(g_ffn[0]), w_router, row(b_router))

    plan, pos = _moe_plan(eid[:, :2])
    x_sorted = _sort_rows(plan, xt_rows, D_MODEL)
    y_rows = _experts(plan, x_sorted, w_gate[0], w_up[0], w_down[0])
    y_prompt, y_sample = _combine(h, y_rows, pos, gates, row(g_final))

    keep = min(B_PAST_REACH, SEQ)
    kcols = slice(B_WIDTH, 2 * B_WIDTH)
    vcols = slice(2 * B_WIDTH, 3 * B_WIDTH)
    tail = slice(T_PROMPT - keep, T_PROMPT)
    new = slice(T_PROMPT, T_ALL)
    heads = lambda t, lead: t.reshape(*lead, B_HEADS, B_HEAD_DIM)
    return (y_prompt.reshape(1, SEQ, D_MODEL),
            y_sample.reshape(DEC_BATCH, DEC_SEQ, D_MODEL),
            heads(qkv32[tail, kcols], (1, 1, keep)),
            heads(qkv32[tail, vcols], (1, 1, keep)),
            heads(qkv32[new, kcols], (1, DEC_BATCH, DEC_SEQ)),
            heads(qkv32[new, vcols], (1, DEC_BATCH, DEC_SEQ)),
            uva[new, A_WIDTH:].reshape(1, DEC_BATCH, DEC_SEQ, A_WIDTH))
```
